```python
import jax, jax.numpy as jnp
from jax import lax
import numpy as np

D_MODEL = 2048
BATCH = 2
SEQ = 8192
DEPTH = 1
DEC_BATCH = 1
DEC_SEQ = 8192
PAST_LEN = 128

CONV_WIDTH = D_MODEL // 2
CONV_GROUPS = 8
CONV_K = 3
SGU_WIDTH = D_MODEL // 2
SGU_HEADS = 8
SGU_HEAD_DIM = SGU_WIDTH // SGU_HEADS
CHUNK = 128
MIX_WIDTH = CONV_WIDTH + SGU_WIDTH
PROJ_WIDTH = 3 * CONV_WIDTH + 2 * SGU_WIDTH
N_EXPERTS = 64
TOP_K = 8
N_EXPERT_GROUPS = 8
TOPK_GROUPS = 4
EXPERT_FF = 512
SHARED_FF = 512
ROUTED_SCALE = 2.5
MOE_BLOCK = 256
EPS = 1e-6

kernel_name = "hybrid_conv_sgu_moe_encoder"


def rms_norm(x, g):
    xf = x.astype(jnp.float32)
    y = xf * lax.rsqrt(jnp.mean(xf * xf, axis=-1, keepdims=True) + EPS)
    return (y * g.astype(jnp.float32)).astype(x.dtype)


def modulate(h, shift, scale):
    return h * (1 + scale[:, None, :]) + shift[:, None, :]


def swiglu(x, wg, wu, wd):
    return (jax.nn.silu(x @ wg) * (x @ wu)) @ wd


def short_conv_mixer(b_gate, c_gate, xh, conv_w):
    z = c_gate * xh
    zp = jnp.pad(z, ((0, 0), (1, 1), (0, 0)))
    conv = zp[:, :-2] * conv_w[0] + zp[:, 1:-1] * conv_w[1] + zp[:, 2:] * conv_w[2]
    return b_gate * conv


def spatial_gating_mixer(u, v, ln_g, ln_b, w_s, b_s):
    bn, s, _ = v.shape
    u = jax.nn.gelu(u)
    v = jax.nn.gelu(v)
    vh = v.reshape(bn, s // CHUNK, CHUNK, SGU_HEADS, SGU_HEAD_DIM)
    vf = vh.astype(jnp.float32)
    mu = jnp.mean(vf, axis=-1, keepdims=True)
    var = jnp.mean(jnp.square(vf - mu), axis=-1, keepdims=True)
    vn = ((vf - mu) * lax.rsqrt(var + EPS)).astype(v.dtype)
    vn = vn * ln_g.reshape(SGU_HEADS, SGU_HEAD_DIM) + ln_b.reshape(SGU_HEADS, SGU_HEAD_DIM)
    sp = jnp.einsum('hpq,bcqhd->bcphd', w_s, vn) + b_s.T[None, None, :, :, None]
    return u * sp.reshape(bn, s, SGU_WIDTH)


def route(h, w_r, e_bias):
    t = h.shape[0]
    scores = jax.nn.sigmoid(jnp.einsum('td,de->te', h.astype(jnp.float32), w_r.astype(jnp.float32)))
    sel = scores + e_bias.astype(jnp.float32)[None, :]
    grp = sel.reshape(t, N_EXPERT_GROUPS, N_EXPERTS // N_EXPERT_GROUPS)
    grp_score = jnp.sum(lax.top_k(grp, 2)[0], axis=-1)
    _, top_g = lax.top_k(grp_score, TOPK_GROUPS)
    gmask = jnp.any(top_g[..., None] == jnp.arange(N_EXPERT_GROUPS)[None, None, :], axis=1)
    emask = jnp.repeat(gmask, N_EXPERTS // N_EXPERT_GROUPS, axis=-1)
    masked = jnp.where(emask, sel, -jnp.inf)
    _, idx = lax.top_k(masked, TOP_K)
    w = jnp.take_along_axis(scores, idx, axis=-1)
    w = w / (jnp.sum(w, axis=-1, keepdims=True) + 1e-20) * ROUTED_SCALE
    return idx, w


def routed_experts(x, idx, gates, wg, wu, wd):
    t, d = x.shape
    tk = t * TOP_K
    flat_e = idx.reshape(tk)
    order = jnp.argsort(flat_e)
    e_sorted = flat_e[order]
    tok_sorted = order // TOP_K
    gate_sorted = gates.reshape(tk)[order]
    counts = jnp.bincount(flat_e, length=N_EXPERTS)
    padded = (counts + MOE_BLOCK - 1) // MOE_BLOCK * MOE_BLOCK
    start = jnp.cumsum(counts) - counts
    pad_end = jnp.cumsum(padded)
    pad_start = pad_end - padded
    dest = pad_start[e_sorted] + jnp.arange(tk) - start[e_sorted]
    n_blocks = -(-tk // MOE_BLOCK) + N_EXPERTS
    slot_tok = jnp.full((n_blocks * MOE_BLOCK,), t, dtype=tok_sorted.dtype).at[dest].set(tok_sorted)
    block_e = jnp.minimum(jnp.searchsorted(pad_end, jnp.arange(n_blocks) * MOE_BLOCK, side='right'),
                          N_EXPERTS - 1)
    x_pad = jnp.concatenate([x, jnp.zeros((1, d), x.dtype)], axis=0)
    xb = x_pad[slot_tok].reshape(n_blocks, MOE_BLOCK, d)

    def expert_block(args):
        xblk, e = args
        return swiglu(xblk, wg[e], wu[e], wd[e])

    yb = lax.map(expert_block, (xb, block_e)).reshape(n_blocks * MOE_BLOCK, d)
    y_sorted = yb[dest] * gate_sorted[:, None].astype(x.dtype)
    return jnp.zeros((t, d), x.dtype).at[tok_sorted].add(y_sorted)


def encoder_layer(x, c, w_ada, b_ada, g_pre_mix, w_in, conv_w, ln_v_g, ln_v_b, w_spatial, b_spatial,
                  g_out_a, g_out_b, w_out, g_post_mix, g_pre_ffn, w_router, router_bias,
                  w_exp_gate, w_exp_up, w_exp_down, w_sh_gate, w_sh_up, w_sh_down, g_post_ffn):
    bn, s, d = x.shape
    mod = jax.nn.silu(c) @ w_ada + b_ada
    sh1, sc1, gt1, sh2, sc2, gt2 = jnp.split(mod, 6, axis=-1)
    h = modulate(rms_norm(x, g_pre_mix), sh1, sc1)
    proj = h @ w_in
    b_g, c_g, xh, u, v = jnp.split(
        proj, [CONV_WIDTH, 2 * CONV_WIDTH, 3 * CONV_WIDTH, 3 * CONV_WIDTH + SGU_WIDTH], axis=-1)
    ya = rms_norm(short_conv_mixer(b_g, c_g, xh, conv_w), g_out_a)
    yb = rms_norm(spatial_gating_mixer(u, v, ln_v_g, ln_v_b, w_spatial, b_spatial), g_out_b)
    mix = jnp.concatenate([ya, yb], axis=-1) @ w_out
    x = x + gt1[:, None, :] * rms_norm(mix, g_post_mix)
    h = modulate(rms_norm(x, g_pre_ffn), sh2, sc2).reshape(bn * s, d)
    idx, gates = route(h, w_router, router_bias)
    ffn = routed_experts(h, idx, gates, w_exp_gate, w_exp_up, w_exp_down) + swiglu(h, w_sh_gate, w_sh_up, w_sh_down)
    x = x + gt2[:, None, :] * rms_norm(ffn.reshape(bn, s, d), g_post_ffn)
    return x


def setup_inputs(seed: int = 0) -> dict:
    key = jax.random.key(seed)
    ks = jax.random.split(key, 32)
    f32 = jnp.float32
    L, D = DEPTH, D_MODEL

    def nrm(k, shape, scale):
        return jax.random.normal(k, shape, f32) * scale

    def gain(k, shape):
        return 1.0 + 0.01 * jax.random.normal(k, shape, f32)

    return {
        "x_prompt": nrm(ks[0], (BATCH, SEQ, D), 1.0),
        "x_sample": nrm(ks[1], (DEC_BATCH, DEC_SEQ, D), 1.0),
        "c_prompt": nrm(ks[2], (BATCH, D), 1.0),
        "c_sample": nrm(ks[3], (DEC_BATCH, D), 1.0),
        "w_ada": nrm(ks[4], (L, D, 6 * D), 0.5 * D ** -0.5),
        "b_ada": nrm(ks[5], (L, 6 * D), 0.01),
        "g_pre_mix": gain(ks[6], (L, D)),
        "w_in": nrm(ks[7], (L, D, PROJ_WIDTH), D ** -0.5),
        "conv_w": nrm(ks[8], (L, CONV_K, CONV_WIDTH), CONV_K ** -0.5),
        "ln_v_g": gain(ks[9], (L, SGU_WIDTH)),
        "ln_v_b": nrm(ks[10], (L, SGU_WIDTH), 0.01),
        "w_spatial": nrm(ks[11], (L, SGU_HEADS, CHUNK, CHUNK), 0.5 * CHUNK ** -0.5),
        "b_spatial": gain(ks[12], (L, SGU_HEADS, CHUNK)),
        "g_out_a": gain(ks[13], (L, CONV_WIDTH)),
        "g_out_b": gain(ks[14], (L, SGU_WIDTH)),
        "w_out": nrm(ks[15], (L, MIX_WIDTH, D), MIX_WIDTH ** -0.5),
        "g_post_mix": gain(ks[16], (L, D)),
        "g_pre_ffn": gain(ks[17], (L, D)),
        "w_router": nrm(ks[18], (L, D, N_EXPERTS), D ** -0.5),
        "router_bias": nrm(ks[19], (L, N_EXPERTS), 0.01),
        "w_exp_gate": nrm(ks[20], (L, N_EXPERTS, D, EXPERT_FF), D ** -0.5),
        "w_exp_up": nrm(ks[21], (L, N_EXPERTS, D, EXPERT_FF), D ** -0.5),
        "w_exp_down": nrm(ks[22], (L, N_EXPERTS, EXPERT_FF, D), EXPERT_FF ** -0.5),
        "w_sh_gate": nrm(ks[23], (L, D, SHARED_FF), D ** -0.5),
        "w_sh_up": nrm(ks[24], (L, D, SHARED_FF), D ** -0.5),
        "w_sh_down": nrm(ks[25], (L, SHARED_FF, D), SHARED_FF ** -0.5),
        "g_post_ffn": gain(ks[26], (L, D)),
    }


def reference(x_prompt, x_sample, c_prompt, c_sample, w_ada, b_ada, g_pre_mix, w_in, conv_w, ln_v_g,
              ln_v_b, w_spatial, b_spatial, g_out_a, g_out_b, w_out, g_post_mix, g_pre_ffn, w_router,
              router_bias, w_exp_gate, w_exp_up, w_exp_down, w_sh_gate, w_sh_up, w_sh_down, g_post_ffn):
    y_prompt = x_prompt
    y_sample = x_sample
    for l in range(DEPTH):
        layer_w = (w_ada[l], b_ada[l], g_pre_mix[l], w_in[l], conv_w[l], ln_v_g[l], ln_v_b[l],
                   w_spatial[l], b_spatial[l], g_out_a[l], g_out_b[l], w_out[l], g_post_mix[l],
                   g_pre_ffn[l], w_router[l], router_bias[l], w_exp_gate[l], w_exp_up[l], w_exp_down[l],
                   w_sh_gate[l], w_sh_up[l], w_sh_down[l], g_post_ffn[l])
        y_prompt = encoder_layer(y_prompt, c_prompt, *layer_w)
        y_sample = encoder_layer(y_sample, c_sample, *layer_w)
    return (y_prompt, y_sample)
```

```python
import functools

import jax
import jax.numpy as jnp
from jax import lax
from jax.experimental import pallas as pl
from jax.experimental.pallas import tpu as pltpu

F32 = jnp.float32
BF16 = jnp.bfloat16
I32 = jnp.int32

D_MODEL = 2048
CONV_W = 1024
SGU_W = 1024
HEADS = 8
HEAD_DIM = 128
CHUNK = 128
N_EXP = 64
TOP_K = 8
N_GRP = 8
GRP_SZ = 8
TOP_GRP = 4
FF = 512
EPS = 1e-6
ROUTED_SCALE = 2.5
LANES = 128
ROW_SUB = D_MODEL // LANES
HALO = 8

VMEM_LIMIT = 56 * 1024 * 1024


def _rms(x, g):
    return x * lax.rsqrt(jnp.mean(x * x, axis=-1, keepdims=True) + EPS) * g


def _dot(a, b):
    return jnp.dot(a, b, preferred_element_type=F32)


def _const_spec(shape):
    nd = len(shape)
    return pl.BlockSpec(shape, lambda *_: (0,) * nd, pipeline_mode=pl.Buffered(1))


def _ada_kernel(c_ref, w_ref, b_ref, o_ref):
    c = c_ref[...]
    s = c * jax.nn.sigmoid(c)
    o_ref[...] = jnp.dot(s, w_ref[...], precision=lax.Precision.HIGHEST,
                         preferred_element_type=F32) + b_ref[...]


def _ada(c_pad, w_ada, b_ada):
    n = w_ada.shape[1]
    bn = 1536
    return pl.pallas_call(
        _ada_kernel,
        grid=(n // bn,),
        in_specs=[pl.BlockSpec(c_pad.shape, lambda j: (0, 0)),
                  pl.BlockSpec((D_MODEL, bn), lambda j: (0, j)),
                  pl.BlockSpec((1, bn), lambda j: (0, j))],
        out_specs=pl.BlockSpec((c_pad.shape[0], bn), lambda j: (0, j)),
        out_shape=jax.ShapeDtypeStruct((c_pad.shape[0], n), F32),
        compiler_params=pltpu.CompilerParams(vmem_limit_bytes=VMEM_LIMIT),
        name="ada_mod",
    )(c_pad, w_ada, b_ada.reshape(1, n))


def _mix_kernel(x_ref, xp_ref, xn_ref, mod_ref, gpre_ref, win_ref, cw_ref, lng_ref, lnb_ref,
                wsp_ref, bsp_ref, ga_ref, gb_ref, wout_ref, gpost_ref, o_ref, zbuf):
    i = pl.program_id(1)
    n_i = pl.num_programs(1)
    tm = x_ref.shape[1]
    sh1 = mod_ref[0, 0:1, :]
    sc1 = mod_ref[0, 1:2, :]
    gt1 = mod_ref[0, 2:3, :]
    gpre = gpre_ref[...]

    def prenorm(xv):
        return _rms(xv, gpre) * (1.0 + sc1) + sh1

    x = x_ref[0]
    h = jnp.concatenate([prenorm(x), prenorm(xp_ref[0]), prenorm(xn_ref[0])], axis=0).astype(BF16)
    ht = h[:tm]

    cz = _dot(h, win_ref[:, CONV_W:3 * CONV_W])
    z = cz[:, :CONV_W] * cz[:, CONV_W:]
    zbuf[HALO:HALO + tm, :] = z[:tm]
    zbuf[0:HALO, :] = jnp.where(i == 0, 0.0, z[tm:tm + HALO])
    zbuf[HALO + tm:, :] = jnp.where(i == n_i - 1, 0.0, z[tm + HALO:])
    cw = cw_ref[...]
    conv = (zbuf[HALO - 1:HALO - 1 + tm, :] * cw[0:1] + zbuf[HALO:HALO + tm, :] * cw[1:2]
            + zbuf[HALO + 1:HALO + 1 + tm, :] * cw[2:3])
    bg = _dot(ht, win_ref[:, :CONV_W])
    ya = _rms(bg * conv, ga_ref[...])

    uv = _dot(ht, win_ref[:, 3 * CONV_W:])
    u = jax.nn.gelu(uv[:, :SGU_W], approximate=True)
    v = jax.nn.gelu(uv[:, SGU_W:], approximate=True)
    lng = lng_ref[...]
    lnb = lnb_ref[...]
    nchunk = tm // CHUNK
    parts = []
    for hd in range(HEADS):
        lo, hi = hd * HEAD_DIM, (hd + 1) * HEAD_DIM
        vh = v[:, lo:hi]
        mu = jnp.mean(vh, axis=-1, keepdims=True)
        vc = vh - mu
        var = jnp.mean(vc * vc, axis=-1, keepdims=True)
        vn = vc * lax.rsqrt(var + EPS) * lng[:, lo:hi] + lnb[:, lo:hi]
        vcat = jnp.concatenate([vn[c * CHUNK:(c + 1) * CHUNK] for c in range(nchunk)],
                               axis=1).astype(BF16)
        sp = _dot(wsp_ref[hd], vcat) + bsp_ref[:, hd:hd + 1]
        sp_rows = jnp.concatenate([sp[:, c * HEAD_DIM:(c + 1) * HEAD_DIM] for c in range(nchunk)],
                                  axis=0)
        parts.append(u[:, lo:hi] * sp_rows)
    yb = _rms(jnp.concatenate(parts, axis=1), gb_ref[...])

    cat = jnp.concatenate([ya, yb], axis=1).astype(BF16)
    mix = _dot(cat, wout_ref[...])
    o_ref[0] = x + gt1 * _rms(mix, gpost_ref[...])


def _mix(x, mod, w, tm):
    b, s, d = x.shape
    n_i = s // tm
    hb = tm // HALO
    n_hb = s // HALO
    row = lambda a: a.reshape(1, -1)
    return pl.pallas_call(
        _mix_kernel,
        grid=(b, n_i),
        in_specs=[
            pl.BlockSpec((1, tm, d), lambda bb, i: (bb, i, 0)),
            pl.BlockSpec((1, HALO, d), lambda bb, i: (bb, jnp.maximum(i * hb - 1, 0), 0)),
            pl.BlockSpec((1, HALO, d), lambda bb, i: (bb, jnp.minimum((i + 1) * hb, n_hb - 1), 0)),
            pl.BlockSpec((1, 6, d), lambda bb, i: (bb, 0, 0)),
            _const_spec((1, d)),
            _const_spec(w["w_in"].shape),
            _const_spec((3, CONV_W)),
            _const_spec((1, SGU_W)),
            _const_spec((1, SGU_W)),
            _const_spec((HEADS, CHUNK, CHUNK)),
            _const_spec((CHUNK, HEADS)),
            _const_spec((1, CONV_W)),
            _const_spec((1, SGU_W)),
            _const_spec((d, d)),
            _const_spec((1, d)),
        ],
        out_specs=pl.BlockSpec((1, tm, d), lambda bb, i: (bb, i, 0)),
        out_shape=jax.ShapeDtypeStruct((b, s, d), F32),
        scratch_shapes=[pltpu.VMEM((tm + 2 * HALO, CONV_W), F32)],
        compiler_params=pltpu.CompilerParams(
            dimension_semantics=("arbitrary", "arbitrary"), vmem_limit_bytes=VMEM_LIMIT),
        name="token_mix",
    )(x, x, x, mod, row(w["g_pre_mix"]), w["w_in"], w["conv_w"], row(w["ln_v_g"]), row(w["ln_v_b"]),
      w["w_spatial"], w["b_spatial_t"], row(w["g_out_a"]), row(w["g_out_b"]), w["w_out"],
      row(w["g_post_mix"]))


def _router_kernel(x_ref, mod_ref, g_ref, wr_ref, br_ref, cin_ref,
                   h2_ref, idx_ref, pos_ref, gate_ref, cnt_ref, carry):
    first = jnp.logical_and(pl.program_id(0) == 0, pl.program_id(1) == 0)

    @pl.when(first)
    def _():
        carry[...] = cin_ref[...]

    tm = x_ref.shape[1]
    sh2 = mod_ref[0, 3:4, :]
    sc2 = mod_ref[0, 4:5, :]
    h2 = _rms(x_ref[0], g_ref[...]) * (1.0 + sc2) + sh2
    for s in range(ROW_SUB):
        h2_ref[pl.ds(s, tm, stride=ROW_SUB), :] = h2[:, s * LANES:(s + 1) * LANES]

    logits = lax.dot_general(wr_ref[...], h2, (((1,), (1,)), ((), ())),
                             precision=lax.Precision.HIGHEST, preferred_element_type=F32)
    scores = jax.nn.sigmoid(logits)
    sel = scores + br_ref[...]

    m1 = sel[0:N_GRP]
    m2 = jnp.full_like(m1, -jnp.inf)
    for j in range(1, GRP_SZ):
        vj = sel[j * N_GRP:(j + 1) * N_GRP]
        m2 = jnp.maximum(m2, jnp.minimum(m1, vj))
        m1 = jnp.maximum(m1, vj)
    gs = m1 + m2
    gi = lax.broadcasted_iota(I32, gs.shape, 0)
    grank = jnp.zeros(gs.shape, I32)
    for g2 in range(N_GRP):
        rowv = gs[g2:g2 + 1, :]
        beats = jnp.logical_or(rowv > gs, jnp.logical_and(rowv == gs, g2 < gi))
        grank = grank + beats.astype(I32)
    gmask = grank < TOP_GRP
    masked = jnp.concatenate(
        [jnp.where(gmask, sel[j * N_GRP:(j + 1) * N_GRP], -jnp.inf) for j in range(GRP_SZ)], axis=0)

    ri = lax.broadcasted_iota(I32, masked.shape, 0)
    e_map = (ri % N_GRP) * GRP_SZ + ri // N_GRP
    rank = jnp.zeros(masked.shape, I32)
    for r2 in range(N_EXP):
        e2 = (r2 % N_GRP) * GRP_SZ + r2 // N_GRP
        rowv = masked[r2:r2 + 1, :]
        beats = jnp.logical_or(rowv > masked, jnp.logical_and(rowv == masked, e2 < e_map))
        rank = rank + beats.astype(I32)
    selected = rank < TOP_K
    self32 = selected.astype(F32)
    wsum = jnp.sum(jnp.where(selected, scores, 0.0), axis=0, keepdims=True)
    gate_full = jnp.where(selected, scores / (wsum + 1e-20) * ROUTED_SCALE, 0.0)

    selb = self32.astype(BF16)
    tr = lax.broadcasted_iota(I32, (tm, tm), 0)
    tc = lax.broadcasted_iota(I32, (tm, tm), 1)
    upper = (tr <= tc).astype(BF16)
    incl = _dot(selb, upper)
    pos_full = incl - self32 + carry[...]
    carry[...] = carry[...] + incl[:, tm - 1:tm]

    lr = lax.broadcasted_iota(I32, (N_EXP, N_EXP), 0)
    lc = lax.broadcasted_iota(I32, (N_EXP, N_EXP), 1)
    lower = (lc < lr).astype(BF16)
    krank = _dot(lower, selb)
    e_f = e_map.astype(F32)
    idx_rows, pos_rows, gate_rows = [], [], []
    for k in range(TOP_K):
        pick = jnp.logical_and(selected, krank == float(k))
        idx_rows.append(jnp.sum(jnp.where(pick, e_f, 0.0), axis=0, keepdims=True))
        pos_rows.append(jnp.sum(jnp.where(pick, pos_full, 0.0), axis=0, keepdims=True))
        gate_rows.append(jnp.sum(jnp.where(pick, gate_full, 0.0), axis=0, keepdims=True))
    idx_ref[...] = jnp.concatenate(idx_rows, axis=0).astype(I32)
    pos_ref[...] = jnp.concatenate(pos_rows, axis=0).astype(I32)
    gate_ref[...] = jnp.concatenate(gate_rows, axis=0)
    cnt_ref[...] = jnp.broadcast_to(carry[...], cnt_ref.shape)


def _router(x1, mod, g_pre_ffn, wr_t, br, count_in, tm):
    b, s, d = x1.shape
    n_i = s // tm
    t = b * s
    return pl.pallas_call(
        _router_kernel,
        grid=(b, n_i),
        in_specs=[
            pl.BlockSpec((1, tm, d), lambda bb, i: (bb, i, 0)),
            pl.BlockSpec((1, 6, d), lambda bb, i: (bb, 0, 0)),
            _const_spec((1, d)),
            _const_spec((N_EXP, d)),
            _const_spec((N_EXP, 1)),
            _const_spec((N_EXP, 1)),
        ],
        out_specs=[
            pl.BlockSpec((tm * ROW_SUB, LANES), lambda bb, i: (bb * n_i + i, 0)),
            pl.BlockSpec((TOP_K, tm), lambda bb, i: (0, bb * n_i + i)),
            pl.BlockSpec((TOP_K, tm), lambda bb, i: (0, bb * n_i + i)),
            pl.BlockSpec((TOP_K, tm), lambda bb, i: (0, bb * n_i + i)),
            pl.BlockSpec((N_EXP, LANES), lambda bb, i: (0, 0)),
        ],
        out_shape=[
            jax.ShapeDtypeStruct((t * ROW_SUB, LANES), F32),
            jax.ShapeDtypeStruct((TOP_K, t), I32),
            jax.ShapeDtypeStruct((TOP_K, t), I32),
            jax.ShapeDtypeStruct((TOP_K, t), F32),
            jax.ShapeDtypeStruct((N_EXP, LANES), F32),
        ],
        scratch_shapes=[pltpu.VMEM((N_EXP, 1), F32)],
        compiler_params=pltpu.CompilerParams(
            dimension_semantics=("arbitrary", "arbitrary"), vmem_limit_bytes=VMEM_LIMIT),
        name="router",
    )(x1, mod, g_pre_ffn.reshape(1, d), wr_t, br, count_in)


def _dispatch_kernel(dest_ref, hp_ref, hs_ref, xs_ref, sem, *, n_prompt_steps, td):
    step = pl.program_id(0)

    def issue(src_ref, tok0):
        def body(j, c):
            src = src_ref.at[pl.ds((tok0 + j) * ROW_SUB, ROW_SUB)]
            for k in range(TOP_K):
                dst = dest_ref[0, 0, j * TOP_K + k]
                pltpu.make_async_copy(src, xs_ref.at[pl.ds(dst * ROW_SUB, ROW_SUB)], sem).start()
            return c
        lax.fori_loop(0, td, body, 0)

    @pl.when(step < n_prompt_steps)
    def _():
        issue(hp_ref, step * td)

    @pl.when(step >= n_prompt_steps)
    def _():
        issue(hs_ref, (step - n_prompt_steps) * td)

    n_rows = td * TOP_K * ROW_SUB
    pltpu.make_async_copy(xs_ref.at[pl.ds(0, n_rows)], xs_ref.at[pl.ds(0, n_rows)], sem).wait()


def _dispatch(dest_flat, h2p, h2s, n_slots, td):
    tp = h2p.shape[0] // ROW_SUB
    ts = h2s.shape[0] // ROW_SUB
    n_steps = (tp + ts) // td
    dest3 = dest_flat.reshape(n_steps, 1, td * TOP_K)
    return pl.pallas_call(
        functools.partial(_dispatch_kernel, n_prompt_steps=tp // td, td=td),
        grid=(n_steps,),
        in_specs=[pl.BlockSpec((1, 1, td * TOP_K), lambda i: (i, 0, 0), memory_space=pltpu.SMEM),
                  pl.BlockSpec(memory_space=pl.ANY),
                  pl.BlockSpec(memory_space=pl.ANY)],
        out_specs=pl.BlockSpec(memory_space=pl.ANY),
        out_shape=jax.ShapeDtypeStruct((n_slots * ROW_SUB, LANES), F32),
        scratch_shapes=[pltpu.SemaphoreType.DMA],
        compiler_params=pltpu.CompilerParams(dimension_semantics=("arbitrary",)),
        name="dispatch",
    )(dest3, h2p, h2s)


def _expert_kernel(be_ref, nv_ref, nu_ref, xs_ref, wg_ref, wu_ref, wd_ref, ys_ref,
                   wg_s, wu_s, wd_s):
    i = pl.program_id(0)
    tme = xs_ref.shape[0] // ROW_SUB

    @pl.when(i < nu_ref[0])
    def _():
        prev = be_ref[jnp.maximum(i - 1, 0)]
        changed = jnp.logical_or(i == 0, prev != be_ref[i])

        @pl.when(changed)
        def _():
            wg_s[...] = wg_ref[0].astype(BF16)
            wu_s[...] = wu_ref[0].astype(BF16)
            wd_s[...] = wd_ref[0].astype(BF16)

        x = jnp.concatenate([xs_ref[pl.ds(s, tme, stride=ROW_SUB), :] for s in range(ROW_SUB)],
                            axis=1)
        rows = lax.broadcasted_iota(I32, (tme, 1), 0)
        x = jnp.where(rows < nv_ref[i], x, 0.0).astype(BF16)
        g = _dot(x, wg_s[...])
        u = _dot(x, wu_s[...])
        a = (g * jax.nn.sigmoid(g) * u).astype(BF16)
        y = _dot(a, wd_s[...])
        for s in range(ROW_SUB):
            ys_ref[pl.ds(s, tme, stride=ROW_SUB), :] = y[:, s * LANES:(s + 1) * LANES]


def _experts(block_e, nvalid, nused, xs, wg, wu, wd, tme):
    n_blocks = xs.shape[0] // (tme * ROW_SUB)
    blk = lambda i, be, nv, nu: (jnp.minimum(i, nu[0] - 1), 0)
    wsel = lambda i, be, nv, nu: (be[i], 0, 0)
    return pl.pallas_call(
        _expert_kernel,
        grid_spec=pltpu.PrefetchScalarGridSpec(
            num_scalar_prefetch=3,
            grid=(n_blocks,),
            in_specs=[pl.BlockSpec((tme * ROW_SUB, LANES), blk),
                      pl.BlockSpec((1, D_MODEL, FF), wsel),
                      pl.BlockSpec((1, D_MODEL, FF), wsel),
                      pl.BlockSpec((1, FF, D_MODEL), wsel)],
            out_specs=pl.BlockSpec((tme * ROW_SUB, LANES), blk),
            scratch_shapes=[pltpu.VMEM((D_MODEL, FF), BF16),
                            pltpu.VMEM((D_MODEL, FF), BF16),
                            pltpu.VMEM((FF, D_MODEL), BF16)],
        ),
        out_shape=jax.ShapeDtypeStruct(xs.shape, F32),
        compiler_params=pltpu.CompilerParams(
            dimension_semantics=("arbitrary",), vmem_limit_bytes=VMEM_LIMIT),
        name="experts",
    )(block_e, nvalid, nused, xs, wg, wu, wd)


def _combine_kernel(dest_ref, x_ref, mod_ref, gffn_ref, gpost_ref, gate_ref, wg_ref, wu_ref, wd_ref,
                    ys_ref, o_ref, buf, sem):
    tc = x_ref.shape[1]

    def body(j, c):
        for k in range(TOP_K):
            src = dest_ref[0, 0, j * TOP_K + k]
            pltpu.make_async_copy(ys_ref.at[pl.ds(src * ROW_SUB, ROW_SUB)],
                                  buf.at[pl.ds((j * TOP_K + k) * ROW_SUB, ROW_SUB)], sem).start()
        return c
    lax.fori_loop(0, tc, body, 0)

    sh2 = mod_ref[0, 3:4, :]
    sc2 = mod_ref[0, 4:5, :]
    gt2 = mod_ref[0, 5:6, :]
    x1 = x_ref[0]
    hb = (_rms(x1, gffn_ref[...]) * (1.0 + sc2) + sh2).astype(BF16)
    g = _dot(hb, wg_ref[...])
    u = _dot(hb, wu_ref[...])
    shared = _dot((g * jax.nn.sigmoid(g) * u).astype(BF16), wd_ref[...])

    n_rows = tc * TOP_K * ROW_SUB
    pltpu.make_async_copy(ys_ref.at[pl.ds(0, n_rows)], buf, sem).wait()

    gate = gate_ref[...]
    stride = TOP_K * ROW_SUB
    parts = []
    for s in range(ROW_SUB):
        acc = gate[:, 0:1] * buf[pl.ds(s, tc, stride=stride), :]
        for k in range(1, TOP_K):
            acc = acc + gate[:, k:k + 1] * buf[pl.ds(k * ROW_SUB + s, tc, stride=stride), :]
        parts.append(acc)
    routed = jnp.concatenate(parts, axis=1)
    ffn = routed + shared
    o_ref[0] = x1 + gt2 * _rms(ffn, gpost_ref[...])


def _combine(dest_flat, gates_tk, tok0, x1, mod, w, ys, tc):
    b, s, d = x1.shape
    n_i = s // tc
    t_all = dest_flat.shape[0] // TOP_K
    dest3 = dest_flat.reshape(t_all // tc, 1, tc * TOP_K)
    t0 = tok0 // tc
    return pl.pallas_call(
        _combine_kernel,
        grid=(b, n_i),
        in_specs=[
            pl.BlockSpec((1, 1, tc * TOP_K), lambda bb, i: (t0 + bb * n_i + i, 0, 0),
                         memory_space=pltpu.SMEM),
            pl.BlockSpec((1, tc, d), lambda bb, i: (bb, i, 0)),
            pl.BlockSpec((1, 6, d), lambda bb, i: (bb, 0, 0)),
            _const_spec((1, d)),
            _const_spec((1, d)),
            pl.BlockSpec((tc, TOP_K), lambda bb, i: (t0 + bb * n_i + i, 0)),
            _const_spec((d, FF)),
            _const_spec((d, FF)),
            _const_spec((FF, d)),
            pl.BlockSpec(memory_space=pl.ANY),
        ],
        out_specs=pl.BlockSpec((1, tc, d), lambda bb, i: (bb, i, 0)),
        out_shape=jax.ShapeDtypeStruct((b, s, d), F32),
        scratch_shapes=[pltpu.VMEM((tc * TOP_K * ROW_SUB, LANES), F32),
                        pltpu.SemaphoreType.DMA],
        compiler_params=pltpu.CompilerParams(
            dimension_semantics=("arbitrary", "arbitrary"), vmem_limit_bytes=VMEM_LIMIT),
        name="combine",
    )(dest3, x1, mod, w["g_pre_ffn"].reshape(1, d), w["g_post_ffn"].reshape(1, d), gates_tk,
      w["w_sh_gate"], w["w_sh_up"], w["w_sh_down"], ys)


def _tile_sizes(s):
    return dict(tm_mix=min(256, s), tm_route=min(512, s), td=min(256, s), tme=min(512, s),
                tc=min(128, s))


def _layer(x_prompt, x_sample, c_prompt, c_sample, lw):
    bp, s, d = x_prompt.shape
    bs = x_sample.shape[0]
    ts = _tile_sizes(s)

    nb = bp + bs
    nb_pad = -(-nb // 8) * 8
    c_all = jnp.concatenate([c_prompt, c_sample, jnp.zeros((nb_pad - nb, d), F32)], axis=0)
    mod = _ada(c_all, lw["w_ada"], lw["b_ada"])[:nb].reshape(nb, 6, d)
    mod_p, mod_s = mod[:bp], mod[bp:]

    w = dict(lw)
    w["w_in"] = lw["w_in"].astype(BF16)
    w["w_out"] = lw["w_out"].astype(BF16)
    w["w_spatial"] = lw["w_spatial"].astype(BF16)
    w["b_spatial_t"] = lw["b_spatial"].T
    w["w_sh_gate"] = lw["w_sh_gate"].astype(BF16)
    w["w_sh_up"] = lw["w_sh_up"].astype(BF16)
    w["w_sh_down"] = lw["w_sh_down"].astype(BF16)

    x1p = _mix(x_prompt, mod_p, w, ts["tm_mix"])
    x1s = _mix(x_sample, mod_s, w, ts["tm_mix"])

    perm = (jnp.arange(N_EXP) % N_GRP) * GRP_SZ + jnp.arange(N_EXP) // N_GRP
    wr_t = lw["w_router"].T[perm]
    br = lw["router_bias"][perm].reshape(N_EXP, 1)
    zero_cnt = jnp.zeros((N_EXP, 1), F32)
    h2p, idx_p, pos_p, gate_p, cnt_p = _router(x1p, mod_p, lw["g_pre_ffn"], wr_t, br, zero_cnt,
                                               ts["tm_route"])
    h2s, idx_s, pos_s, gate_s, cnt_all = _router(x1s, mod_s, lw["g_pre_ffn"], wr_t, br,
                                                 cnt_p[:, 0:1], ts["tm_route"])

    tme = ts["tme"]
    t_all = (bp + bs) * s
    counts = cnt_all[:, 0].astype(I32).reshape(GRP_SZ, N_GRP).T.reshape(N_EXP)
    padded = (counts + tme - 1) // tme * tme
    pad_end = jnp.cumsum(padded)
    off = pad_end - padded
    idx = jnp.concatenate([idx_p, idx_s], axis=1)
    pos = jnp.concatenate([pos_p, pos_s], axis=1)
    gates_tk = jnp.concatenate([gate_p, gate_s], axis=1).T
    dest_flat = (jnp.take(off, idx) + pos).T.reshape(t_all * TOP_K)
    n_blocks = -(-t_all * TOP_K // tme) + N_EXP
    nused = (pad_end[-1] // tme).astype(I32)
    blk_ids = jnp.minimum(jnp.arange(n_blocks, dtype=I32), nused - 1)
    block_e = jnp.minimum(jnp.searchsorted(pad_end, blk_ids * tme, side="right"),
                          N_EXP - 1).astype(I32)
    nvalid = jnp.clip(counts[block_e] - (blk_ids * tme - off[block_e]), 0, tme).astype(I32)

    xs = _dispatch(dest_flat, h2p, h2s, n_blocks * tme, ts["td"])
    ys = _experts(block_e, nvalid, nused.reshape(1), xs, lw["w_exp_gate"], lw["w_exp_up"],
                  lw["w_exp_down"], tme)

    y_p = _combine(dest_flat, gates_tk, 0, x1p, mod_p, w, ys, ts["tc"])
    y_s = _combine(dest_flat, gates_tk, bp * s, x1s, mod_s, w, ys, ts["tc"])
    return y_p, y_s


def kernel(x_prompt, x_sample, c_prompt, c_sample, w_ada, b_ada, g_pre_mix, w_in, conv_w, ln_v_g, ln_v_b, w_spatial, b_spatial, g_out_a, g_out_b, w_out, g_post_mix, g_pre_ffn, w_router, router_bias, w_exp_gate, w_exp_up, w_exp_down, w_sh_gate, w_sh_up, w_sh_down, g_post_ffn):
    names = ("w_ada", "b_ada", "g_pre_mix", "w_in", "conv_w", "ln_v_g", "ln_v_b", "w_spatial",
             "b_spatial", "g_out_a", "g_out_b", "w_out", "g_post_mix", "g_pre_ffn", "w_router",
             "router_bias", "w_exp_gate", "w_exp_up", "w_exp_down", "w_sh_gate", "w_sh_up",
             "w_sh_down", "g_post_ffn")
    stacked = (w_ada, b_ada, g_pre_mix, w_in, conv_w, ln_v_g, ln_v_b, w_spatial, b_spatial, g_out_a,
               g_out_b, w_out, g_post_mix, g_pre_ffn, w_router, router_bias, w_exp_gate, w_exp_up,
               w_exp_down, w_sh_gate, w_sh_up, w_sh_down, g_post_ffn)
    y_p, y_s = x_prompt, x_sample
    for l in range(w_ada.shape[0]):
        lw = {n: a[l] for n, a in zip(names, stacked)}
        y_p, y_s = _layer(y_p, y_s, c_prompt, c_sample, lw)
    return (y_p, y_s)
```

```python
import functools

import jax
import jax.numpy as jnp
from jax import lax
from jax.experimental import pallas as pl
from jax.experimental.pallas import tpu as pltpu

F32 = jnp.float32
BF16 = jnp.bfloat16
I32 = jnp.int32

D_MODEL = 2048
CONV_W = 1024
SGU_W = 1024
HEADS = 8
HEAD_DIM = 128
CHUNK = 128
N_EXP = 64
TOP_K = 8
N_GRP = 8
GRP_SZ = 8
TOP_GRP = 4
FF = 512
EPS = 1e-6
ROUTED_SCALE = 2.5
LANES = 128
ROW_SUB = D_MODEL // LANES
HALO = 8

VMEM_LIMIT = 56 * 1024 * 1024


def _rms(x, g):
    return x * lax.rsqrt(jnp.mean(x * x, axis=-1, keepdims=True) + EPS) * g


def _dot(a, b):
    return jnp.dot(a, b, preferred_element_type=F32)


def _const_spec(shape):
    nd = len(shape)
    return pl.BlockSpec(shape, lambda *_: (0,) * nd, pipeline_mode=pl.Buffered(1))


def _ada_kernel(c_ref, w_ref, b_ref, o_ref):
    c = c_ref[...]
    s = c * jax.nn.sigmoid(c)
    o_ref[...] = jnp.dot(s, w_ref[...], precision=lax.Precision.HIGHEST,
                         preferred_element_type=F32) + b_ref[...]


def _ada(c_pad, w_ada, b_ada):
    n = w_ada.shape[1]
    bn = 1536
    return pl.pallas_call(
        _ada_kernel,
        grid=(n // bn,),
        in_specs=[pl.BlockSpec(c_pad.shape, lambda j: (0, 0)),
                  pl.BlockSpec((D_MODEL, bn), lambda j: (0, j)),
                  pl.BlockSpec((1, bn), lambda j: (0, j))],
        out_specs=pl.BlockSpec((c_pad.shape[0], bn), lambda j: (0, j)),
        out_shape=jax.ShapeDtypeStruct((c_pad.shape[0], n), F32),
        compiler_params=pltpu.CompilerParams(vmem_limit_bytes=VMEM_LIMIT),
        name="ada_mod",
    )(c_pad, w_ada, b_ada.reshape(1, n))


def _mix_kernel(x_ref, xp_ref, xn_ref, mod_ref, gpre_ref, win_ref, cw_ref, lng_ref, lnb_ref,
                wsp_ref, bsp_ref, ga_ref, gb_ref, wout_ref, gpost_ref, o_ref, zbuf):
    i = pl.program_id(1)
    n_i = pl.num_programs(1)
    tm = x_ref.shape[1]
    sh1 = mod_ref[0, 0:1, :]
    sc1 = mod_ref[0, 1:2, :]
    gt1 = mod_ref[0, 2:3, :]
    gpre = gpre_ref[...]

    def prenorm(xv):
        return _rms(xv, gpre) * (1.0 + sc1) + sh1

    x = x_ref[0]
    h = jnp.concatenate([prenorm(x), prenorm(xp_ref[0]), prenorm(xn_ref[0])], axis=0).astype(BF16)
    ht = h[:tm]

    cz = _dot(h, win_ref[:, CONV_W:3 * CONV_W])
    z = cz[:, :CONV_W] * cz[:, CONV_W:]
    zbuf[HALO:HALO + tm, :] = z[:tm]
    zbuf[0:HALO, :] = jnp.where(i == 0, 0.0, z[tm:tm + HALO])
    zbuf[HALO + tm:, :] = jnp.where(i == n_i - 1, 0.0, z[tm + HALO:])
    cw = cw_ref[...]
    conv = (zbuf[HALO - 1:HALO - 1 + tm, :] * cw[0:1] + zbuf[HALO:HALO + tm, :] * cw[1:2]
            + zbuf[HALO + 1:HALO + 1 + tm, :] * cw[2:3])
    bg = _dot(ht, win_ref[:, :CONV_W])
    ya = _rms(bg * conv, ga_ref[...])

    uv = _dot(ht, win_ref[:, 3 * CONV_W:])
    u = jax.nn.gelu(uv[:, :SGU_W], approximate=True)
    v = jax.nn.gelu(uv[:, SGU_W:], approximate=True)
    lng = lng_ref[...]
    lnb = lnb_ref[...]
    nchunk = tm // CHUNK
    parts = []
    for hd in range(HEADS):
        lo, hi = hd * HEAD_DIM, (hd + 1) * HEAD_DIM
        vh = v[:, lo:hi]
        mu = jnp.mean(vh, axis=-1, keepdims=True)
        vc = vh - mu
        var = jnp.mean(vc * vc, axis=-1, keepdims=True)
        vn = vc * lax.rsqrt(var + EPS) * lng[:, lo:hi] + lnb[:, lo:hi]
        vcat = jnp.concatenate([vn[c * CHUNK:(c + 1) * CHUNK] for c in range(nchunk)],
                               axis=1).astype(BF16)
        sp = _dot(wsp_ref[hd], vcat) + bsp_ref[:, hd:hd + 1]
        sp_rows = jnp.concatenate([sp[:, c * HEAD_DIM:(c + 1) * HEAD_DIM] for c in range(nchunk)],
                                  axis=0)
        parts.append(u[:, lo:hi] * sp_rows)
    yb = _rms(jnp.concatenate(parts, axis=1), gb_ref[...])

    cat = jnp.concatenate([ya, yb], axis=1).astype(BF16)
    mix = _dot(cat, wout_ref[...])
    o_ref[0] = x + gt1 * _rms(mix, gpost_ref[...])


def _mix(x, mod, w, tm):
    b, s, d = x.shape
    n_i = s // tm
    hb = tm // HALO
    n_hb = s // HALO
    row = lambda a: a.reshape(1, -1)
    return pl.pallas_call(
        _mix_kernel,
        grid=(b, n_i),
        in_specs=[
            pl.BlockSpec((1, tm, d), lambda bb, i: (bb, i, 0)),
            pl.BlockSpec((1, HALO, d), lambda bb, i: (bb, jnp.maximum(i * hb - 1, 0), 0)),
            pl.BlockSpec((1, HALO, d), lambda bb, i: (bb, jnp.minimum((i + 1) * hb, n_hb - 1), 0)),
            pl.BlockSpec((1, 6, d), lambda bb, i: (bb, 0, 0)),
            _const_spec((1, d)),
            _const_spec(w["w_in"].shape),
            _const_spec((3, CONV_W)),
            _const_spec((1, SGU_W)),
            _const_spec((1, SGU_W)),
            _const_spec((HEADS, CHUNK, CHUNK)),
            _const_spec((CHUNK, HEADS)),
            _const_spec((1, CONV_W)),
            _const_spec((1, SGU_W)),
            _const_spec((d, d)),
            _const_spec((1, d)),
        ],
        out_specs=pl.BlockSpec((1, tm, d), lambda bb, i: (bb, i, 0)),
        out_shape=jax.ShapeDtypeStruct((b, s, d), F32),
        scratch_shapes=[pltpu.VMEM((tm + 2 * HALO, CONV_W), F32)],
        compiler_params=pltpu.CompilerParams(
            dimension_semantics=("arbitrary", "arbitrary"), vmem_limit_bytes=VMEM_LIMIT),
        name="token_mix",
    )(x, x, x, mod, row(w["g_pre_mix"]), w["w_in"], w["conv_w"], row(w["ln_v_g"]), row(w["ln_v_b"]),
      w["w_spatial"], w["b_spatial_t"], row(w["g_out_a"]), row(w["g_out_b"]), w["w_out"],
      row(w["g_post_mix"]))


def _router_kernel(x_ref, mod_ref, g_ref, wr_ref, br_ref, cin_ref,
                   h2_ref, idx_ref, pos_ref, gate_ref, cnt_ref, carry):
    first = jnp.logical_and(pl.program_id(0) == 0, pl.program_id(1) == 0)

    @pl.when(first)
    def _():
        carry[...] = cin_ref[...]

    tm = x_ref.shape[1]
    sh2 = mod_ref[0, 3:4, :]
    sc2 = mod_ref[0, 4:5, :]
    h2 = _rms(x_ref[0], g_ref[...]) * (1.0 + sc2) + sh2
    for s in range(ROW_SUB):
        h2_ref[pl.ds(s, tm, stride=ROW_SUB), :] = h2[:, s * LANES:(s + 1) * LANES]

    logits = lax.dot_general(wr_ref[...], h2, (((1,), (1,)), ((), ())),
                             precision=lax.Precision.HIGHEST, preferred_element_type=F32)
    scores = jax.nn.sigmoid(logits)
    sel = scores + br_ref[...]

    m1 = sel[0:N_GRP]
    m2 = jnp.full_like(m1, -jnp.inf)
    for j in range(1, GRP_SZ):
        vj = sel[j * N_GRP:(j + 1) * N_GRP]
        m2 = jnp.maximum(m2, jnp.minimum(m1, vj))
        m1 = jnp.maximum(m1, vj)
    gs = m1 + m2
    gi = lax.broadcasted_iota(I32, gs.shape, 0)
    grank = jnp.zeros(gs.shape, I32)
    for g2 in range(N_GRP):
        rowv = gs[g2:g2 + 1, :]
        beats = jnp.logical_or(rowv > gs, jnp.logical_and(rowv == gs, g2 < gi))
        grank = grank + beats.astype(I32)
    gmask = grank < TOP_GRP
    masked = jnp.concatenate(
        [jnp.where(gmask, sel[j * N_GRP:(j + 1) * N_GRP], -jnp.inf) for j in range(GRP_SZ)], axis=0)

    ri = lax.broadcasted_iota(I32, masked.shape, 0)
    e_map = (ri % N_GRP) * GRP_SZ + ri // N_GRP
    rank = jnp.zeros(masked.shape, I32)
    for r2 in range(N_EXP):
        e2 = (r2 % N_GRP) * GRP_SZ + r2 // N_GRP
        rowv = masked[r2:r2 + 1, :]
        beats = jnp.logical_or(rowv > masked, jnp.logical_and(rowv == masked, e2 < e_map))
        rank = rank + beats.astype(I32)
    selected = rank < TOP_K
    self32 = selected.astype(F32)
    wsum = jnp.sum(jnp.where(selected, scores, 0.0), axis=0, keepdims=True)
    gate_full = jnp.where(selected, scores / (wsum + 1e-20) * ROUTED_SCALE, 0.0)

    selb = self32.astype(BF16)
    tr = lax.broadcasted_iota(I32, (tm, tm), 0)
    tc = lax.broadcasted_iota(I32, (tm, tm), 1)
    upper = (tr <= tc).astype(BF16)
    incl = _dot(selb, upper)
    pos_full = incl - self32 + carry[...]
    carry[...] = carry[...] + incl[:, tm - 1:tm]

    lr = lax.broadcasted_iota(I32, (N_EXP, N_EXP), 0)
    lc = lax.broadcasted_iota(I32, (N_EXP, N_EXP), 1)
    lower = (lc < lr).astype(BF16)
    krank = _dot(lower, selb)
    e_f = e_map.astype(F32)
    idx_rows, pos_rows, gate_rows = [], [], []
    for k in range(TOP_K):
        pick = jnp.logical_and(selected, krank == float(k))
        idx_rows.append(jnp.sum(jnp.where(pick, e_f, 0.0), axis=0, keepdims=True))
        pos_rows.append(jnp.sum(jnp.where(pick, pos_full, 0.0), axis=0, keepdims=True))
        gate_rows.append(jnp.sum(jnp.where(pick, gate_full, 0.0), axis=0, keepdims=True))
    idx_ref[...] = jnp.concatenate(idx_rows, axis=0).astype(I32)
    pos_ref[...] = jnp.concatenate(pos_rows, axis=0).astype(I32)
    gate_ref[...] = jnp.concatenate(gate_rows, axis=0)
    cnt_ref[...] = jnp.broadcast_to(carry[...], cnt_ref.shape)


def _router(x1, mod, g_pre_ffn, wr_t, br, count_in, tm):
    b, s, d = x1.shape
    n_i = s // tm
    t = b * s
    return pl.pallas_call(
        _router_kernel,
        grid=(b, n_i),
        in_specs=[
            pl.BlockSpec((1, tm, d), lambda bb, i: (bb, i, 0)),
            pl.BlockSpec((1, 6, d), lambda bb, i: (bb, 0, 0)),
            _const_spec((1, d)),
            _const_spec((N_EXP, d)),
            _const_spec((N_EXP, 1)),
            _const_spec((N_EXP, 1)),
        ],
        out_specs=[
            pl.BlockSpec((tm * ROW_SUB, LANES), lambda bb, i: (bb * n_i + i, 0)),
            pl.BlockSpec((TOP_K, tm), lambda bb, i: (0, bb * n_i + i)),
            pl.BlockSpec((TOP_K, tm), lambda bb, i: (0, bb * n_i + i)),
            pl.BlockSpec((TOP_K, tm), lambda bb, i: (0, bb * n_i + i)),
            pl.BlockSpec((N_EXP, LANES), lambda bb, i: (0, 0)),
        ],
        out_shape=[
            jax.ShapeDtypeStruct((t * ROW_SUB, LANES), F32),
            jax.ShapeDtypeStruct((TOP_K, t), I32),
            jax.ShapeDtypeStruct((TOP_K, t), I32),
            jax.ShapeDtypeStruct((TOP_K, t), F32),
            jax.ShapeDtypeStruct((N_EXP, LANES), F32),
        ],
        scratch_shapes=[pltpu.VMEM((N_EXP, 1), F32)],
        compiler_params=pltpu.CompilerParams(
            dimension_semantics=("arbitrary", "arbitrary"), vmem_limit_bytes=VMEM_LIMIT),
        name="router",
    )(x1, mod, g_pre_ffn.reshape(1, d), wr_t, br, count_in)


def _dispatch_kernel(dest_ref, hp_ref, hs_ref, xs_ref, sem, *, n_prompt_steps, td):
    step = pl.program_id(0)

    def issue(src_ref):
        def body(j, c):
            src = src_ref.at[pl.ds(j * ROW_SUB, ROW_SUB)]
            for k in range(TOP_K):
                dst = dest_ref[0, 0, j * TOP_K + k]
                pltpu.make_async_copy(src, xs_ref.at[pl.ds(dst * ROW_SUB, ROW_SUB)], sem).start()
            return c
        lax.fori_loop(0, td, body, 0)

    @pl.when(step < n_prompt_steps)
    def _():
        issue(hp_ref)

    @pl.when(step >= n_prompt_steps)
    def _():
        issue(hs_ref)

    n_rows = td * TOP_K * ROW_SUB
    pltpu.make_async_copy(xs_ref.at[pl.ds(0, n_rows)], xs_ref.at[pl.ds(0, n_rows)], sem).wait()


def _dispatch(dest_flat, h2p, h2s, n_slots, td):
    tp = h2p.shape[0] // ROW_SUB
    ts = h2s.shape[0] // ROW_SUB
    np_steps = tp // td
    ns_steps = ts // td
    n_steps = np_steps + ns_steps
    dest3 = dest_flat.reshape(n_steps, 1, td * TOP_K)
    return pl.pallas_call(
        functools.partial(_dispatch_kernel, n_prompt_steps=np_steps, td=td),
        grid=(n_steps,),
        in_specs=[pl.BlockSpec((1, 1, td * TOP_K), lambda i: (i, 0, 0), memory_space=pltpu.SMEM),
                  pl.BlockSpec((td * ROW_SUB, LANES),
                               lambda i: (jnp.minimum(i, np_steps - 1), 0)),
                  pl.BlockSpec((td * ROW_SUB, LANES),
                               lambda i: (jnp.maximum(i - np_steps, 0), 0))],
        out_specs=pl.BlockSpec(memory_space=pl.ANY),
        out_shape=jax.ShapeDtypeStruct((n_slots * ROW_SUB, LANES), F32),
        scratch_shapes=[pltpu.SemaphoreType.DMA],
        compiler_params=pltpu.CompilerParams(dimension_semantics=("arbitrary",)),
        name="dispatch",
    )(dest3, h2p, h2s)


def _expert_kernel(be_ref, nv_ref, nu_ref, xs_ref, wg_ref, wu_ref, wd_ref, ys_ref,
                   wg_s, wu_s, wd_s):
    i = pl.program_id(0)
    tme = xs_ref.shape[0] // ROW_SUB

    @pl.when(i < nu_ref[0])
    def _():
        prev = be_ref[jnp.maximum(i - 1, 0)]
        changed = jnp.logical_or(i == 0, prev != be_ref[i])

        @pl.when(changed)
        def _():
            wg_s[...] = wg_ref[0].astype(BF16)
            wu_s[...] = wu_ref[0].astype(BF16)
            wd_s[...] = wd_ref[0].astype(BF16)

        x = jnp.concatenate([xs_ref[pl.ds(s, tme, stride=ROW_SUB), :] for s in range(ROW_SUB)],
                            axis=1)
        rows = lax.broadcasted_iota(I32, (tme, 1), 0)
        x = jnp.where(rows < nv_ref[i], x, 0.0).astype(BF16)
        g = _dot(x, wg_s[...])
        u = _dot(x, wu_s[...])
        a = (g * jax.nn.sigmoid(g) * u).astype(BF16)
        y = _dot(a, wd_s[...])
        for s in range(ROW_SUB):
            ys_ref[pl.ds(s, tme, stride=ROW_SUB), :] = y[:, s * LANES:(s + 1) * LANES]


def _experts(block_e, nvalid, nused, xs, wg, wu, wd, tme):
    n_blocks = xs.shape[0] // (tme * ROW_SUB)
    blk = lambda i, be, nv, nu: (jnp.minimum(i, nu[0] - 1), 0)
    wsel = lambda i, be, nv, nu: (be[i], 0, 0)
    return pl.pallas_call(
        _expert_kernel,
        grid_spec=pltpu.PrefetchScalarGridSpec(
            num_scalar_prefetch=3,
            grid=(n_blocks,),
            in_specs=[pl.BlockSpec((tme * ROW_SUB, LANES), blk),
                      pl.BlockSpec((1, D_MODEL, FF), wsel),
                      pl.BlockSpec((1, D_MODEL, FF), wsel),
                      pl.BlockSpec((1, FF, D_MODEL), wsel)],
            out_specs=pl.BlockSpec((tme * ROW_SUB, LANES), blk),
            scratch_shapes=[pltpu.VMEM((D_MODEL, FF), BF16),
                            pltpu.VMEM((D_MODEL, FF), BF16),
                            pltpu.VMEM((FF, D_MODEL), BF16)],
        ),
        out_shape=jax.ShapeDtypeStruct(xs.shape, F32),
        compiler_params=pltpu.CompilerParams(
            dimension_semantics=("arbitrary",), vmem_limit_bytes=VMEM_LIMIT),
        name="experts",
    )(block_e, nvalid, nused, xs, wg, wu, wd)


def _combine_kernel(dest_ref, x_ref, mod_ref, gffn_ref, gpost_ref, gate_ref, wg_ref, wu_ref, wd_ref,
                    ys_ref, o_ref, buf, sem):
    tc = x_ref.shape[1]

    def body(j, c):
        for k in range(TOP_K):
            src = dest_ref[0, 0, j * TOP_K + k]
            pltpu.make_async_copy(ys_ref.at[pl.ds(src * ROW_SUB, ROW_SUB)],
                                  buf.at[pl.ds((j * TOP_K + k) * ROW_SUB, ROW_SUB)], sem).start()
        return c
    lax.fori_loop(0, tc, body, 0)

    sh2 = mod_ref[0, 3:4, :]
    sc2 = mod_ref[0, 4:5, :]
    gt2 = mod_ref[0, 5:6, :]
    x1 = x_ref[0]
    hb = (_rms(x1, gffn_ref[...]) * (1.0 + sc2) + sh2).astype(BF16)
    g = _dot(hb, wg_ref[...])
    u = _dot(hb, wu_ref[...])
    shared = _dot((g * jax.nn.sigmoid(g) * u).astype(BF16), wd_ref[...])

    n_rows = tc * TOP_K * ROW_SUB
    pltpu.make_async_copy(ys_ref.at[pl.ds(0, n_rows)], buf, sem).wait()

    gate = gate_ref[...]
    stride = TOP_K * ROW_SUB
    parts = []
    for s in range(ROW_SUB):
        acc = gate[:, 0:1] * buf[pl.ds(s, tc, stride=stride), :]
        for k in range(1, TOP_K):
            acc = acc + gate[:, k:k + 1] * buf[pl.ds(k * ROW_SUB + s, tc, stride=stride), :]
        parts.append(acc)
    routed = jnp.concatenate(parts, axis=1)
    ffn = routed + shared
    o_ref[0] = x1 + gt2 * _rms(ffn, gpost_ref[...])


def _combine(dest_flat, gates_tk, tok0, x1, mod, w, ys, tc):
    b, s, d = x1.shape
    n_i = s // tc
    t_all = dest_flat.shape[0] // TOP_K
    dest3 = dest_flat.reshape(t_all // tc, 1, tc * TOP_K)
    t0 = tok0 // tc
    return pl.pallas_call(
        _combine_kernel,
        grid=(b, n_i),
        in_specs=[
            pl.BlockSpec((1, 1, tc * TOP_K), lambda bb, i: (t0 + bb * n_i + i, 0, 0),
                         memory_space=pltpu.SMEM),
            pl.BlockSpec((1, tc, d), lambda bb, i: (bb, i, 0)),
            pl.BlockSpec((1, 6, d), lambda bb, i: (bb, 0, 0)),
            _const_spec((1, d)),
            _const_spec((1, d)),
            pl.BlockSpec((tc, TOP_K), lambda bb, i: (t0 + bb * n_i + i, 0)),
            _const_spec((d, FF)),
            _const_spec((d, FF)),
            _const_spec((FF, d)),
            pl.BlockSpec(memory_space=pl.ANY),
        ],
        out_specs=pl.BlockSpec((1, tc, d), lambda bb, i: (bb, i, 0)),
        out_shape=jax.ShapeDtypeStruct((b, s, d), F32),
        scratch_shapes=[pltpu.VMEM((tc * TOP_K * ROW_SUB, LANES), F32),
                        pltpu.SemaphoreType.DMA],
        compiler_params=pltpu.CompilerParams(
            dimension_semantics=("arbitrary", "arbitrary"), vmem_limit_bytes=VMEM_LIMIT),
        name="combine",
    )(dest3, x1, mod, w["g_pre_ffn"].reshape(1, d), w["g_post_ffn"].reshape(1, d), gates_tk,
      w["w_sh_gate"], w["w_sh_up"], w["w_sh_down"], ys)


def _tile_sizes(s):
    return dict(tm_mix=min(256, s), tm_route=min(512, s), td=min(256, s), tme=min(512, s),
                tc=min(128, s))


def _layer(x_prompt, x_sample, c_prompt, c_sample, lw):
    bp, s, d = x_prompt.shape
    bs = x_sample.shape[0]
    ts = _tile_sizes(s)

    nb = bp + bs
    nb_pad = -(-nb // 8) * 8
    c_all = jnp.concatenate([c_prompt, c_sample, jnp.zeros((nb_pad - nb, d), F32)], axis=0)
    mod = _ada(c_all, lw["w_ada"], lw["b_ada"])[:nb].reshape(nb, 6, d)
    mod_p, mod_s = mod[:bp], mod[bp:]

    w = dict(lw)
    w["w_in"] = lw["w_in"].astype(BF16)
    w["w_out"] = lw["w_out"].astype(BF16)
    w["w_spatial"] = lw["w_spatial"].astype(BF16)
    w["b_spatial_t"] = lw["b_spatial"].T
    w["w_sh_gate"] = lw["w_sh_gate"].astype(BF16)
    w["w_sh_up"] = lw["w_sh_up"].astype(BF16)
    w["w_sh_down"] = lw["w_sh_down"].astype(BF16)

    x1p = _mix(x_prompt, mod_p, w, ts["tm_mix"])
    x1s = _mix(x_sample, mod_s, w, ts["tm_mix"])

    perm = (jnp.arange(N_EXP) % N_GRP) * GRP_SZ + jnp.arange(N_EXP) // N_GRP
    wr_t = lw["w_router"].T[perm]
    br = lw["router_bias"][perm].reshape(N_EXP, 1)
    zero_cnt = jnp.zeros((N_EXP, 1), F32)
    h2p, idx_p, pos_p, gate_p, cnt_p = _router(x1p, mod_p, lw["g_pre_ffn"], wr_t, br, zero_cnt,
                                               ts["tm_route"])
    h2s, idx_s, pos_s, gate_s, cnt_all = _router(x1s, mod_s, lw["g_pre_ffn"], wr_t, br,
                                                 cnt_p[:, 0:1], ts["tm_route"])

    tme = ts["tme"]
    t_all = (bp + bs) * s
    counts = cnt_all[:, 0].astype(I32).reshape(GRP_SZ, N_GRP).T.reshape(N_EXP)
    padded = (counts + tme - 1) // tme * tme
    pad_end = jnp.cumsum(padded)
    off = pad_end - padded
    idx = jnp.concatenate([idx_p, idx_s], axis=1)
    pos = jnp.concatenate([pos_p, pos_s], axis=1)
    gates_tk = jnp.concatenate([gate_p, gate_s], axis=1).T
    e_ids = jnp.arange(N_EXP, dtype=I32)
    off_of_idx = jnp.sum(jnp.where(idx[:, :, None] == e_ids, off, 0), axis=-1)
    dest_flat = (off_of_idx + pos).T.reshape(t_all * TOP_K)
    n_blocks = -(-t_all * TOP_K // tme) + N_EXP
    nused = (pad_end[-1] // tme).astype(I32)
    blk_ids = jnp.minimum(jnp.arange(n_blocks, dtype=I32), nused - 1)
    blk_start = blk_ids * tme
    block_e = jnp.minimum(jnp.sum((pad_end[None, :] <= blk_start[:, None]).astype(I32), axis=1),
                          N_EXP - 1)
    be_hot = block_e[:, None] == e_ids
    cnt_of_blk = jnp.sum(jnp.where(be_hot, counts, 0), axis=1)
    off_of_blk = jnp.sum(jnp.where(be_hot, off, 0), axis=1)
    nvalid = jnp.clip(cnt_of_blk - (blk_start - off_of_blk), 0, tme).astype(I32)

    xs = _dispatch(dest_flat, h2p, h2s, n_blocks * tme, ts["td"])
    ys = _experts(block_e, nvalid, nused.reshape(1), xs, lw["w_exp_gate"], lw["w_exp_up"],
                  lw["w_exp_down"], tme)

    y_p = _combine(dest_flat, gates_tk, 0, x1p, mod_p, w, ys, ts["tc"])
    y_s = _combine(dest_flat, gates_tk, bp * s, x1s, mod_s, w, ys, ts["tc"])
    return y_p, y_s


def kernel(x_prompt, x_sample, c_prompt, c_sample, w_ada, b_ada, g_pre_mix, w_in, conv_w, ln_v_g, ln_v_b, w_spatial, b_spatial, g_out_a, g_out_b, w_out, g_post_mix, g_pre_ffn, w_router, router_bias, w_exp_gate, w_exp_up, w_exp_down, w_sh_gate, w_sh_up, w_sh_down, g_post_ffn):
    names = ("w_ada", "b_ada", "g_pre_mix", "w_in", "conv_w", "ln_v_g", "ln_v_b", "w_spatial",
             "b_spatial", "g_out_a", "g_out_b", "w_out", "g_post_mix", "g_pre_ffn", "w_router",
             "router_bias", "w_exp_gate", "w_exp_up", "w_exp_down", "w_sh_gate", "w_sh_up",
             "w_sh_down", "g_post_ffn")
    stacked = (w_ada, b_ada, g_pre_mix, w_in, conv_w, ln_v_g, ln_v_b, w_spatial, b_spatial, g_out_a,
               g_out_b, w_out, g_post_mix, g_pre_ffn, w_router, router_bias, w_exp_gate, w_exp_up,
               w_exp_down, w_sh_gate, w_sh_up, w_sh_down, g_post_ffn)
    y_p, y_s = x_prompt, x_sample
    for l in range(w_ada.shape[0]):
        lw = {n: a[l] for n, a in zip(names, stacked)}
        y_p, y_s = _layer(y_p, y_s, c_prompt, c_sample, lw)
    return (y_p, y_s)
```

```python
import functools

import jax
import jax.numpy as jnp
from jax import lax
from jax.experimental import pallas as pl
from jax.experimental.pallas import tpu as pltpu

F32 = jnp.float32
BF16 = jnp.bfloat16
I32 = jnp.int32

D_MODEL = 2048
CONV_W = 1024
SGU_W = 1024
HEADS = 8
HEAD_DIM = 128
CHUNK = 128
N_EXP = 64
TOP_K = 8
N_GRP = 8
GRP_SZ = 8
TOP_GRP = 4
FF = 512
EPS = 1e-6
ROUTED_SCALE = 2.5
LANES = 128
ROW_SUB = D_MODEL // LANES
HALO = 8

VMEM_LIMIT = 56 * 1024 * 1024


def _rms(x, g):
    return x * lax.rsqrt(jnp.mean(x * x, axis=-1, keepdims=True) + EPS) * g


def _dot(a, b):
    return jnp.dot(a, b, preferred_element_type=F32)


def _const_spec(shape):
    nd = len(shape)
    return pl.BlockSpec(shape, lambda *_: (0,) * nd, pipeline_mode=pl.Buffered(1))


def _ada_kernel(c_ref, w_ref, b_ref, o_ref):
    c = c_ref[...]
    s = c * jax.nn.sigmoid(c)
    o_ref[...] = jnp.dot(s, w_ref[...], precision=lax.Precision.HIGHEST,
                         preferred_element_type=F32) + b_ref[...]


def _ada(c_pad, w_ada, b_ada):
    n = w_ada.shape[1]
    bn = 1536
    return pl.pallas_call(
        _ada_kernel,
        grid=(n // bn,),
        in_specs=[pl.BlockSpec(c_pad.shape, lambda j: (0, 0)),
                  pl.BlockSpec((D_MODEL, bn), lambda j: (0, j)),
                  pl.BlockSpec((1, bn), lambda j: (0, j))],
        out_specs=pl.BlockSpec((c_pad.shape[0], bn), lambda j: (0, j)),
        out_shape=jax.ShapeDtypeStruct((c_pad.shape[0], n), F32),
        compiler_params=pltpu.CompilerParams(vmem_limit_bytes=VMEM_LIMIT),
        name="ada_mod",
    )(c_pad, w_ada, b_ada.reshape(1, n))


def _mix_kernel(x_ref, xp_ref, xn_ref, mod_ref, gpre_ref, win_ref, cw_ref, lng_ref, lnb_ref,
                wsp_ref, bsp_ref, ga_ref, gb_ref, wout_ref, gpost_ref, o_ref, zbuf, *, n_i):
    i = pl.program_id(1)
    tm = x_ref.shape[1]
    sh1 = mod_ref[0, 0:1, :]
    sc1 = mod_ref[0, 1:2, :]
    gt1 = mod_ref[0, 2:3, :]
    gpre = gpre_ref[...]

    def prenorm(xv):
        return _rms(xv, gpre) * (1.0 + sc1) + sh1

    x = x_ref[0]
    h = jnp.concatenate([prenorm(x), prenorm(xp_ref[0]), prenorm(xn_ref[0])], axis=0).astype(BF16)
    ht = h[:tm]

    cz = _dot(h, win_ref[:, CONV_W:3 * CONV_W])
    z = cz[:, :CONV_W] * cz[:, CONV_W:]
    zbuf[HALO:HALO + tm, :] = z[:tm]
    zbuf[0:HALO, :] = jnp.where(i == 0, 0.0, z[tm:tm + HALO])
    zbuf[HALO + tm:, :] = jnp.where(i == n_i - 1, 0.0, z[tm + HALO:])
    cw = cw_ref[...]
    conv = (zbuf[HALO - 1:HALO - 1 + tm, :] * cw[0:1] + zbuf[HALO:HALO + tm, :] * cw[1:2]
            + zbuf[HALO + 1:HALO + 1 + tm, :] * cw[2:3])
    bg = _dot(ht, win_ref[:, :CONV_W])
    ya = _rms(bg * conv, ga_ref[...])

    uv = _dot(ht, win_ref[:, 3 * CONV_W:])
    u = jax.nn.gelu(uv[:, :SGU_W], approximate=True)
    v = jax.nn.gelu(uv[:, SGU_W:], approximate=True)
    lng = lng_ref[...]
    lnb = lnb_ref[...]
    nchunk = tm // CHUNK
    parts = []
    for hd in range(HEADS):
        lo, hi = hd * HEAD_DIM, (hd + 1) * HEAD_DIM
        vh = v[:, lo:hi]
        mu = jnp.mean(vh, axis=-1, keepdims=True)
        vc = vh - mu
        var = jnp.mean(vc * vc, axis=-1, keepdims=True)
        vn = vc * lax.rsqrt(var + EPS) * lng[:, lo:hi] + lnb[:, lo:hi]
        vcat = jnp.concatenate([vn[c * CHUNK:(c + 1) * CHUNK] for c in range(nchunk)],
                               axis=1).astype(BF16)
        sp = _dot(wsp_ref[hd], vcat) + bsp_ref[:, hd:hd + 1]
        sp_rows = jnp.concatenate([sp[:, c * HEAD_DIM:(c + 1) * HEAD_DIM] for c in range(nchunk)],
                                  axis=0)
        parts.append(u[:, lo:hi] * sp_rows)
    yb = _rms(jnp.concatenate(parts, axis=1), gb_ref[...])

    cat = jnp.concatenate([ya, yb], axis=1).astype(BF16)
    mix = _dot(cat, wout_ref[...])
    o_ref[0] = x + gt1 * _rms(mix, gpost_ref[...])


def _mix(x, mod, w, tm):
    b, s, d = x.shape
    n_i = s // tm
    hb = tm // HALO
    n_hb = s // HALO
    row = lambda a: a.reshape(1, -1)
    return pl.pallas_call(
        functools.partial(_mix_kernel, n_i=n_i),
        grid=(b, n_i),
        in_specs=[
            pl.BlockSpec((1, tm, d), lambda bb, i: (bb, i, 0)),
            pl.BlockSpec((1, HALO, d), lambda bb, i: (bb, jnp.maximum(i * hb - 1, 0), 0)),
            pl.BlockSpec((1, HALO, d), lambda bb, i: (bb, jnp.minimum((i + 1) * hb, n_hb - 1), 0)),
            pl.BlockSpec((1, 6, d), lambda bb, i: (bb, 0, 0)),
            _const_spec((1, d)),
            _const_spec(w["w_in"].shape),
            _const_spec((3, CONV_W)),
            _const_spec((1, SGU_W)),
            _const_spec((1, SGU_W)),
            _const_spec((HEADS, CHUNK, CHUNK)),
            _const_spec((CHUNK, HEADS)),
            _const_spec((1, CONV_W)),
            _const_spec((1, SGU_W)),
            _const_spec((d, d)),
            _const_spec((1, d)),
        ],
        out_specs=pl.BlockSpec((1, tm, d), lambda bb, i: (bb, i, 0)),
        out_shape=jax.ShapeDtypeStruct((b, s, d), F32),
        scratch_shapes=[pltpu.VMEM((tm + 2 * HALO, CONV_W), F32)],
        compiler_params=pltpu.CompilerParams(
            dimension_semantics=("arbitrary", "arbitrary"), vmem_limit_bytes=VMEM_LIMIT),
        name="token_mix",
    )(x, x, x, mod, row(w["g_pre_mix"]), w["w_in"], w["conv_w"], row(w["ln_v_g"]), row(w["ln_v_b"]),
      w["w_spatial"], w["b_spatial_t"], row(w["g_out_a"]), row(w["g_out_b"]), w["w_out"],
      row(w["g_post_mix"]))


def _router_kernel(x_ref, mod_ref, g_ref, wr_ref, br_ref, cin_ref,
                   h2_ref, idx_ref, pos_ref, gate_ref, cnt_ref, carry):
    first = jnp.logical_and(pl.program_id(0) == 0, pl.program_id(1) == 0)

    @pl.when(first)
    def _():
        carry[...] = cin_ref[...]

    tm = x_ref.shape[1]
    sh2 = mod_ref[0, 3:4, :]
    sc2 = mod_ref[0, 4:5, :]
    h2 = _rms(x_ref[0], g_ref[...]) * (1.0 + sc2) + sh2
    for s in range(ROW_SUB):
        h2_ref[pl.ds(s, tm, stride=ROW_SUB), :] = h2[:, s * LANES:(s + 1) * LANES]

    logits = lax.dot_general(wr_ref[...], h2, (((1,), (1,)), ((), ())),
                             precision=lax.Precision.HIGHEST, preferred_element_type=F32)
    scores = jax.nn.sigmoid(logits)
    sel = scores + br_ref[...]

    m1 = sel[0:N_GRP]
    m2 = jnp.full_like(m1, -jnp.inf)
    for j in range(1, GRP_SZ):
        vj = sel[j * N_GRP:(j + 1) * N_GRP]
        m2 = jnp.maximum(m2, jnp.minimum(m1, vj))
        m1 = jnp.maximum(m1, vj)
    gs = m1 + m2
    gi = lax.broadcasted_iota(I32, gs.shape, 0)
    grank = jnp.zeros(gs.shape, I32)
    for g2 in range(N_GRP):
        rowv = gs[g2:g2 + 1, :]
        beats = jnp.logical_or(rowv > gs, jnp.logical_and(rowv == gs, g2 < gi))
        grank = grank + beats.astype(I32)
    gmask = grank < TOP_GRP
    masked = jnp.concatenate(
        [jnp.where(gmask, sel[j * N_GRP:(j + 1) * N_GRP], -jnp.inf) for j in range(GRP_SZ)], axis=0)

    ri = lax.broadcasted_iota(I32, masked.shape, 0)
    e_map = (ri % N_GRP) * GRP_SZ + ri // N_GRP
    rank = jnp.zeros(masked.shape, I32)
    for r2 in range(N_EXP):
        e2 = (r2 % N_GRP) * GRP_SZ + r2 // N_GRP
        rowv = masked[r2:r2 + 1, :]
        beats = jnp.logical_or(rowv > masked, jnp.logical_and(rowv == masked, e2 < e_map))
        rank = rank + beats.astype(I32)
    selected = rank < TOP_K
    self32 = selected.astype(F32)
    wsum = jnp.sum(jnp.where(selected, scores, 0.0), axis=0, keepdims=True)
    gate_full = jnp.where(selected, scores / (wsum + 1e-20) * ROUTED_SCALE, 0.0)

    selb = self32.astype(BF16)
    tr = lax.broadcasted_iota(I32, (tm, tm), 0)
    tc = lax.broadcasted_iota(I32, (tm, tm), 1)
    upper = (tr <= tc).astype(BF16)
    incl = _dot(selb, upper)
    pos_full = incl - self32 + carry[...]
    carry[...] = carry[...] + incl[:, tm - 1:tm]

    lr = lax.broadcasted_iota(I32, (N_EXP, N_EXP), 0)
    lc = lax.broadcasted_iota(I32, (N_EXP, N_EXP), 1)
    lower = (lc < lr).astype(BF16)
    krank = _dot(lower, selb)
    e_f = e_map.astype(F32)
    idx_rows, pos_rows, gate_rows = [], [], []
    for k in range(TOP_K):
        pick = jnp.logical_and(selected, krank == float(k))
        idx_rows.append(jnp.sum(jnp.where(pick, e_f, 0.0), axis=0, keepdims=True))
        pos_rows.append(jnp.sum(jnp.where(pick, pos_full, 0.0), axis=0, keepdims=True))
        gate_rows.append(jnp.sum(jnp.where(pick, gate_full, 0.0), axis=0, keepdims=True))
    idx_ref[...] = jnp.concatenate(idx_rows, axis=0).astype(I32)
    pos_ref[...] = jnp.concatenate(pos_rows, axis=0).astype(I32)
    gate_ref[...] = jnp.concatenate(gate_rows, axis=0)
    cnt_ref[...] = jnp.broadcast_to(carry[...], cnt_ref.shape)


def _router(x1, mod, g_pre_ffn, wr_t, br, count_in, tm):
    b, s, d = x1.shape
    n_i = s // tm
    t = b * s
    return pl.pallas_call(
        _router_kernel,
        grid=(b, n_i),
        in_specs=[
            pl.BlockSpec((1, tm, d), lambda bb, i: (bb, i, 0)),
            pl.BlockSpec((1, 6, d), lambda bb, i: (bb, 0, 0)),
            _const_spec((1, d)),
            _const_spec((N_EXP, d)),
            _const_spec((N_EXP, 1)),
            _const_spec((N_EXP, 1)),
        ],
        out_specs=[
            pl.BlockSpec((tm * ROW_SUB, LANES), lambda bb, i: (bb * n_i + i, 0)),
            pl.BlockSpec((TOP_K, tm), lambda bb, i: (0, bb * n_i + i)),
            pl.BlockSpec((TOP_K, tm), lambda bb, i: (0, bb * n_i + i)),
            pl.BlockSpec((TOP_K, tm), lambda bb, i: (0, bb * n_i + i)),
            pl.BlockSpec((N_EXP, LANES), lambda bb, i: (0, 0)),
        ],
        out_shape=[
            jax.ShapeDtypeStruct((t * ROW_SUB, LANES), F32),
            jax.ShapeDtypeStruct((TOP_K, t), I32),
            jax.ShapeDtypeStruct((TOP_K, t), I32),
            jax.ShapeDtypeStruct((TOP_K, t), F32),
            jax.ShapeDtypeStruct((N_EXP, LANES), F32),
        ],
        scratch_shapes=[pltpu.VMEM((N_EXP, 1), F32)],
        compiler_params=pltpu.CompilerParams(
            dimension_semantics=("arbitrary", "arbitrary"), vmem_limit_bytes=VMEM_LIMIT),
        name="router",
    )(x1, mod, g_pre_ffn.reshape(1, d), wr_t, br, count_in)


def _dispatch_kernel(dest_ref, hp_ref, hs_ref, xs_ref, sem, *, n_prompt_steps, td):
    step = pl.program_id(0)

    def issue(src_ref):
        def body(j, c):
            src = src_ref.at[pl.ds(j * ROW_SUB, ROW_SUB)]
            for k in range(TOP_K):
                dst = dest_ref[0, 0, j * TOP_K + k]
                pltpu.make_async_copy(src, xs_ref.at[pl.ds(dst * ROW_SUB, ROW_SUB)], sem).start()
            return c
        lax.fori_loop(0, td, body, 0, unroll=2)

    @pl.when(step < n_prompt_steps)
    def _():
        issue(hp_ref)

    @pl.when(step >= n_prompt_steps)
    def _():
        issue(hs_ref)

    n_rows = td * TOP_K * ROW_SUB
    pltpu.make_async_copy(xs_ref.at[pl.ds(0, n_rows)], xs_ref.at[pl.ds(0, n_rows)], sem).wait()


def _dispatch(dest_flat, h2p, h2s, n_slots, td):
    tp = h2p.shape[0] // ROW_SUB
    ts = h2s.shape[0] // ROW_SUB
    np_steps = tp // td
    ns_steps = ts // td
    n_steps = np_steps + ns_steps
    dest3 = dest_flat.reshape(n_steps, 1, td * TOP_K)
    return pl.pallas_call(
        functools.partial(_dispatch_kernel, n_prompt_steps=np_steps, td=td),
        grid=(n_steps,),
        in_specs=[pl.BlockSpec((1, 1, td * TOP_K), lambda i: (i, 0, 0), memory_space=pltpu.SMEM),
                  pl.BlockSpec((td * ROW_SUB, LANES),
                               lambda i: (jnp.minimum(i, np_steps - 1), 0)),
                  pl.BlockSpec((td * ROW_SUB, LANES),
                               lambda i: (jnp.maximum(i - np_steps, 0), 0))],
        out_specs=pl.BlockSpec(memory_space=pl.ANY),
        out_shape=jax.ShapeDtypeStruct((n_slots * ROW_SUB, LANES), F32),
        scratch_shapes=[pltpu.SemaphoreType.DMA],
        compiler_params=pltpu.CompilerParams(dimension_semantics=("arbitrary",)),
        name="dispatch",
    )(dest3, h2p, h2s)


def _expert_kernel(be_ref, nv_ref, nu_ref, xs_ref, wg_ref, wu_ref, wd_ref, ys_ref,
                   wg_s, wu_s, wd_s):
    i = pl.program_id(0)
    tme = xs_ref.shape[0] // ROW_SUB

    @pl.when(i < nu_ref[0])
    def _():
        prev = be_ref[jnp.maximum(i - 1, 0)]
        changed = jnp.logical_or(i == 0, prev != be_ref[i])

        @pl.when(changed)
        def _():
            wg_s[...] = wg_ref[0].astype(BF16)
            wu_s[...] = wu_ref[0].astype(BF16)
            wd_s[...] = wd_ref[0].astype(BF16)

        x = jnp.concatenate([xs_ref[pl.ds(s, tme, stride=ROW_SUB), :] for s in range(ROW_SUB)],
                            axis=1)
        rows = lax.broadcasted_iota(I32, (tme, 1), 0)
        x = jnp.where(rows < nv_ref[i], x, 0.0).astype(BF16)
        g = _dot(x, wg_s[...])
        u = _dot(x, wu_s[...])
        a = (g * jax.nn.sigmoid(g) * u).astype(BF16)
        y = _dot(a, wd_s[...])
        for s in range(ROW_SUB):
            ys_ref[pl.ds(s, tme, stride=ROW_SUB), :] = y[:, s * LANES:(s + 1) * LANES]


def _experts(block_e, nvalid, nused, xs, wg, wu, wd, tme):
    n_blocks = xs.shape[0] // (tme * ROW_SUB)
    blk = lambda i, be, nv, nu: (jnp.minimum(i, nu[0] - 1), 0)
    wsel = lambda i, be, nv, nu: (be[i], 0, 0)
    return pl.pallas_call(
        _expert_kernel,
        grid_spec=pltpu.PrefetchScalarGridSpec(
            num_scalar_prefetch=3,
            grid=(n_blocks,),
            in_specs=[pl.BlockSpec((tme * ROW_SUB, LANES), blk),
                      pl.BlockSpec((1, D_MODEL, FF), wsel),
                      pl.BlockSpec((1, D_MODEL, FF), wsel),
                      pl.BlockSpec((1, FF, D_MODEL), wsel)],
            out_specs=pl.BlockSpec((tme * ROW_SUB, LANES), blk),
            scratch_shapes=[pltpu.VMEM((D_MODEL, FF), BF16),
                            pltpu.VMEM((D_MODEL, FF), BF16),
                            pltpu.VMEM((FF, D_MODEL), BF16)],
        ),
        out_shape=jax.ShapeDtypeStruct(xs.shape, F32),
        compiler_params=pltpu.CompilerParams(
            dimension_semantics=("arbitrary",), vmem_limit_bytes=VMEM_LIMIT),
        name="experts",
    )(block_e, nvalid, nused, xs, wg, wu, wd)


def _combine_kernel(dest_ref, destn_ref, gsm_ref, x_ref, mod_ref, gffn_ref, gpost_ref,
                    wg_ref, wu_ref, wd_ref, ys_ref, o_ref, buf, rsum, sem, *, n_b, n_i):
    tc = x_ref.shape[1]
    n = pl.program_id(0) * n_i + pl.program_id(1)
    total = n_b * n_i
    slot = n % 2

    def gather_rows(idx_ref, slot_):
        def body(j, c):
            for k in range(TOP_K):
                src = idx_ref[0, 0, j * TOP_K + k]
                pltpu.make_async_copy(ys_ref.at[pl.ds(src * ROW_SUB, ROW_SUB)],
                                      buf.at[slot_, pl.ds((j * TOP_K + k) * ROW_SUB, ROW_SUB)],
                                      sem.at[slot_]).start()
            return c
        lax.fori_loop(0, tc, body, 0, unroll=2)

    @pl.when(n == 0)
    def _():
        gather_rows(dest_ref, 0)

    @pl.when(n + 1 < total)
    def _():
        gather_rows(destn_ref, 1 - slot)

    sh2 = mod_ref[0, 3:4, :]
    sc2 = mod_ref[0, 4:5, :]
    gt2 = mod_ref[0, 5:6, :]
    x1 = x_ref[0]
    hb = (_rms(x1, gffn_ref[...]) * (1.0 + sc2) + sh2).astype(BF16)
    g = _dot(hb, wg_ref[...])
    u = _dot(hb, wu_ref[...])
    shared = _dot((g * jax.nn.sigmoid(g) * u).astype(BF16), wd_ref[...])

    n_rows = tc * TOP_K * ROW_SUB
    pltpu.make_async_copy(ys_ref.at[pl.ds(0, n_rows)], buf.at[slot], sem.at[slot]).wait()

    half = ROW_SUB // 2

    def tok_body(j, c):
        base = j * (TOP_K * ROW_SUB)
        gk = gsm_ref[0, 0, j * TOP_K]
        acc0 = gk * buf[slot, pl.ds(base, half), :]
        acc1 = gk * buf[slot, pl.ds(base + half, half), :]
        for k in range(1, TOP_K):
            gk = gsm_ref[0, 0, j * TOP_K + k]
            acc0 = acc0 + gk * buf[slot, pl.ds(base + k * ROW_SUB, half), :]
            acc1 = acc1 + gk * buf[slot, pl.ds(base + k * ROW_SUB + half, half), :]
        rsum[pl.ds(j * ROW_SUB, half), :] = acc0
        rsum[pl.ds(j * ROW_SUB + half, half), :] = acc1
        return c
    lax.fori_loop(0, tc, tok_body, 0, unroll=2)

    routed = jnp.concatenate([rsum[pl.ds(s, tc, stride=ROW_SUB), :] for s in range(ROW_SUB)], axis=1)
    ffn = routed + shared
    o_ref[0] = x1 + gt2 * _rms(ffn, gpost_ref[...])


def _combine(dest_flat, gates_flat, tok0, x1, mod, w, ys, tc):
    b, s, d = x1.shape
    n_i = s // tc
    t_all = dest_flat.shape[0] // TOP_K
    dest3 = dest_flat.reshape(t_all // tc, 1, tc * TOP_K)
    gates3 = gates_flat.reshape(t_all // tc, 1, tc * TOP_K)
    t0 = tok0 // tc
    last = t0 + b * n_i - 1
    cur = lambda bb, i: (t0 + bb * n_i + i, 0, 0)
    nxt = lambda bb, i: (jnp.minimum(t0 + bb * n_i + i + 1, last), 0, 0)
    return pl.pallas_call(
        functools.partial(_combine_kernel, n_b=b, n_i=n_i),
        grid=(b, n_i),
        in_specs=[
            pl.BlockSpec((1, 1, tc * TOP_K), cur, memory_space=pltpu.SMEM),
            pl.BlockSpec((1, 1, tc * TOP_K), nxt, memory_space=pltpu.SMEM),
            pl.BlockSpec((1, 1, tc * TOP_K), cur, memory_space=pltpu.SMEM),
            pl.BlockSpec((1, tc, d), lambda bb, i: (bb, i, 0)),
            pl.BlockSpec((1, 6, d), lambda bb, i: (bb, 0, 0)),
            _const_spec((1, d)),
            _const_spec((1, d)),
            _const_spec((d, FF)),
            _const_spec((d, FF)),
            _const_spec((FF, d)),
            pl.BlockSpec(memory_space=pl.ANY),
        ],
        out_specs=pl.BlockSpec((1, tc, d), lambda bb, i: (bb, i, 0)),
        out_shape=jax.ShapeDtypeStruct((b, s, d), F32),
        scratch_shapes=[pltpu.VMEM((2, tc * TOP_K * ROW_SUB, LANES), F32),
                        pltpu.VMEM((tc * ROW_SUB, LANES), F32),
                        pltpu.SemaphoreType.DMA((2,))],
        compiler_params=pltpu.CompilerParams(
            dimension_semantics=("arbitrary", "arbitrary"), vmem_limit_bytes=VMEM_LIMIT),
        name="combine",
    )(dest3, dest3, gates3, x1, mod, w["g_pre_ffn"].reshape(1, d), w["g_post_ffn"].reshape(1, d),
      w["w_sh_gate"], w["w_sh_up"], w["w_sh_down"], ys)


def _tile_sizes(s):
    return dict(tm_mix=min(256, s), tm_route=min(512, s), td=min(256, s), tme=min(512, s),
                tc=min(128, s))


def _layer(x_prompt, x_sample, c_prompt, c_sample, lw):
    bp, s, d = x_prompt.shape
    bs = x_sample.shape[0]
    ts = _tile_sizes(s)

    nb = bp + bs
    nb_pad = -(-nb // 8) * 8
    c_all = jnp.concatenate([c_prompt, c_sample, jnp.zeros((nb_pad - nb, d), F32)], axis=0)
    mod = _ada(c_all, lw["w_ada"], lw["b_ada"])[:nb].reshape(nb, 6, d)
    mod_p, mod_s = mod[:bp], mod[bp:]

    w = dict(lw)
    w["w_in"] = lw["w_in"].astype(BF16)
    w["w_out"] = lw["w_out"].astype(BF16)
    w["w_spatial"] = lw["w_spatial"].astype(BF16)
    w["b_spatial_t"] = lw["b_spatial"].T
    w["w_sh_gate"] = lw["w_sh_gate"].astype(BF16)
    w["w_sh_up"] = lw["w_sh_up"].astype(BF16)
    w["w_sh_down"] = lw["w_sh_down"].astype(BF16)

    x1p = _mix(x_prompt, mod_p, w, ts["tm_mix"])
    x1s = _mix(x_sample, mod_s, w, ts["tm_mix"])

    perm = (jnp.arange(N_EXP) % N_GRP) * GRP_SZ + jnp.arange(N_EXP) // N_GRP
    wr_t = lw["w_router"].T[perm]
    br = lw["router_bias"][perm].reshape(N_EXP, 1)
    zero_cnt = jnp.zeros((N_EXP, 1), F32)
    h2p, idx_p, pos_p, gate_p, cnt_p = _router(x1p, mod_p, lw["g_pre_ffn"], wr_t, br, zero_cnt,
                                               ts["tm_route"])
    h2s, idx_s, pos_s, gate_s, cnt_all = _router(x1s, mod_s, lw["g_pre_ffn"], wr_t, br,
                                                 cnt_p[:, 0:1], ts["tm_route"])

    tme = ts["tme"]
    t_all = (bp + bs) * s
    counts = cnt_all[:, 0].astype(I32).reshape(GRP_SZ, N_GRP).T.reshape(N_EXP)
    padded = (counts + tme - 1) // tme * tme
    pad_end = jnp.cumsum(padded)
    off = pad_end - padded
    idx = jnp.concatenate([idx_p, idx_s], axis=1)
    pos = jnp.concatenate([pos_p, pos_s], axis=1)
    gates_flat = jnp.concatenate([gate_p, gate_s], axis=1).T.reshape(t_all * TOP_K)
    e_ids = jnp.arange(N_EXP, dtype=I32)
    off_of_idx = jnp.sum(jnp.where(idx[:, :, None] == e_ids, off, 0), axis=-1)
    dest_flat = (off_of_idx + pos).T.reshape(t_all * TOP_K)
    n_blocks = -(-t_all * TOP_K // tme) + N_EXP
    nused = (pad_end[-1] // tme).astype(I32)
    blk_ids = jnp.minimum(jnp.arange(n_blocks, dtype=I32), nused - 1)
    blk_start = blk_ids * tme
    block_e = jnp.minimum(jnp.sum((pad_end[None, :] <= blk_start[:, None]).astype(I32), axis=1),
                          N_EXP - 1)
    be_hot = block_e[:, None] == e_ids
    cnt_of_blk = jnp.sum(jnp.where(be_hot, counts, 0), axis=1)
    off_of_blk = jnp.sum(jnp.where(be_hot, off, 0), axis=1)
    nvalid = jnp.clip(cnt_of_blk - (blk_start - off_of_blk), 0, tme).astype(I32)

    xs = _dispatch(dest_flat, h2p, h2s, n_blocks * tme, ts["td"])
    ys = _experts(block_e, nvalid, nused.reshape(1), xs, lw["w_exp_gate"], lw["w_exp_up"],
                  lw["w_exp_down"], tme)

    y_p = _combine(dest_flat, gates_flat,0, x1p, mod_p, w, ys, ts["tc"])
    y_s = _combine(dest_flat, gates_flat,bp * s, x1s, mod_s, w, ys, ts["tc"])
    return y_p, y_s


def kernel(x_prompt, x_sample, c_prompt, c_sample, w_ada, b_ada, g_pre_mix, w_in, conv_w, ln_v_g, ln_v_b, w_spatial, b_spatial, g_out_a, g_out_b, w_out, g_post_mix, g_pre_ffn, w_router, router_bias, w_exp_gate, w_exp_up, w_exp_down, w_sh_gate, w_sh_up, w_sh_down, g_post_ffn):
    names = ("w_ada", "b_ada", "g_pre_mix", "w_in", "conv_w", "ln_v_g", "ln_v_b", "w_spatial",
             "b_spatial", "g_out_a", "g_out_b", "w_out", "g_post_mix", "g_pre_ffn", "w_router",
             "router_bias", "w_exp_gate", "w_exp_up", "w_exp_down", "w_sh_gate", "w_sh_up",
             "w_sh_down", "g_post_ffn")
    stacked = (w_ada, b_ada, g_pre_mix, w_in, conv_w, ln_v_g, ln_v_b, w_spatial, b_spatial, g_out_a,
               g_out_b, w_out, g_post_mix, g_pre_ffn, w_router, router_bias, w_exp_gate, w_exp_up,
               w_exp_down, w_sh_gate, w_sh_up, w_sh_down, g_post_ffn)
    y_p, y_s = x_prompt, x_sample
    for l in range(w_ada.shape[0]):
        lw = {n: a[l] for n, a in zip(names, stacked)}
        y_p, y_s = _layer(y_p, y_s, c_prompt, c_sample, lw)
    return (y_p, y_s)
```

```python
import functools

import jax
import jax.numpy as jnp
from jax import lax
from jax.experimental import pallas as pl
from jax.experimental.pallas import tpu as pltpu

F32 = jnp.float32
BF16 = jnp.bfloat16
I32 = jnp.int32

D_MODEL = 2048
CONV_W = 1024
SGU_W = 1024
HEADS = 8
HEAD_DIM = 128
CHUNK = 128
N_EXP = 64
TOP_K = 8
N_GRP = 8
GRP_SZ = 8
TOP_GRP = 4
FF = 512
EPS = 1e-6
ROUTED_SCALE = 2.5
LANES = 128
ROW_SUB = D_MODEL // LANES
HALO = 8

VMEM_LIMIT = 56 * 1024 * 1024


def _rms(x, g):
    return x * lax.rsqrt(jnp.mean(x * x, axis=-1, keepdims=True) + EPS) * g


def _dot(a, b):
    return jnp.dot(a, b, preferred_element_type=F32)


def _const_spec(shape):
    nd = len(shape)
    return pl.BlockSpec(shape, lambda *_: (0,) * nd, pipeline_mode=pl.Buffered(1))


def _ada_kernel(c_ref, w_ref, b_ref, o_ref):
    c = c_ref[...]
    s = c * jax.nn.sigmoid(c)
    o_ref[...] = jnp.dot(s, w_ref[...], precision=lax.Precision.HIGHEST,
                         preferred_element_type=F32) + b_ref[...]


def _ada(c_pad, w_ada, b_ada):
    n = w_ada.shape[1]
    bn = 1536
    return pl.pallas_call(
        _ada_kernel,
        grid=(n // bn,),
        in_specs=[pl.BlockSpec(c_pad.shape, lambda j: (0, 0)),
                  pl.BlockSpec((D_MODEL, bn), lambda j: (0, j)),
                  pl.BlockSpec((1, bn), lambda j: (0, j))],
        out_specs=pl.BlockSpec((c_pad.shape[0], bn), lambda j: (0, j)),
        out_shape=jax.ShapeDtypeStruct((c_pad.shape[0], n), F32),
        compiler_params=pltpu.CompilerParams(vmem_limit_bytes=VMEM_LIMIT),
        name="ada_mod",
    )(c_pad, w_ada, b_ada.reshape(1, n))


def _mix_kernel(x_ref, xp_ref, xn_ref, mod_ref, gpre_ref, win_ref, cw_ref, lng_ref, lnb_ref,
                wsp_ref, bsp_ref, ga_ref, gb_ref, wout_ref, gpost_ref, *rest, n_i):
    o_ref, zbuf = rest[-2:]
    i = pl.program_id(1)
    tm = x_ref.shape[1]
    sh1 = mod_ref[0, 0:1, :]
    sc1 = mod_ref[0, 1:2, :]
    gt1 = mod_ref[0, 2:3, :]
    gpre = gpre_ref[...]

    def prenorm(xv):
        return _rms(xv, gpre) * (1.0 + sc1) + sh1

    x = x_ref[0]
    h = jnp.concatenate([prenorm(x), prenorm(xp_ref[0]), prenorm(xn_ref[0])], axis=0).astype(BF16)
    ht = h[:tm]

    cz = _dot(h, win_ref[:, CONV_W:3 * CONV_W])
    z = cz[:, :CONV_W] * cz[:, CONV_W:]
    zbuf[HALO:HALO + tm, :] = z[:tm]
    zbuf[0:HALO, :] = jnp.where(i == 0, 0.0, z[tm:tm + HALO])
    zbuf[HALO + tm:, :] = jnp.where(i == n_i - 1, 0.0, z[tm + HALO:])
    cw = cw_ref[...]
    conv = (zbuf[HALO - 1:HALO - 1 + tm, :] * cw[0:1] + zbuf[HALO:HALO + tm, :] * cw[1:2]
            + zbuf[HALO + 1:HALO + 1 + tm, :] * cw[2:3])
    bg = _dot(ht, win_ref[:, :CONV_W])
    ya = _rms(bg * conv, ga_ref[...])

    uv = _dot(ht, win_ref[:, 3 * CONV_W:])
    u = jax.nn.gelu(uv[:, :SGU_W], approximate=True)
    v = jax.nn.gelu(uv[:, SGU_W:], approximate=True)
    lng = lng_ref[...]
    lnb = lnb_ref[...]
    nchunk = tm // CHUNK
    parts = []
    for hd in range(HEADS):
        lo, hi = hd * HEAD_DIM, (hd + 1) * HEAD_DIM
        vh = v[:, lo:hi]
        mu = jnp.mean(vh, axis=-1, keepdims=True)
        vc = vh - mu
        var = jnp.mean(vc * vc, axis=-1, keepdims=True)
        vn = vc * lax.rsqrt(var + EPS) * lng[:, lo:hi] + lnb[:, lo:hi]
        vcat = jnp.concatenate([vn[c * CHUNK:(c + 1) * CHUNK] for c in range(nchunk)],
                               axis=1).astype(BF16)
        sp = _dot(wsp_ref[hd], vcat) + bsp_ref[:, hd:hd + 1]
        sp_rows = jnp.concatenate([sp[:, c * HEAD_DIM:(c + 1) * HEAD_DIM] for c in range(nchunk)],
                                  axis=0)
        parts.append(u[:, lo:hi] * sp_rows)
    yb = _rms(jnp.concatenate(parts, axis=1), gb_ref[...])

    cat = jnp.concatenate([ya, yb], axis=1).astype(BF16)
    mix = _dot(cat, wout_ref[...])
    o_ref[0] = x + gt1 * _rms(mix, gpost_ref[...])


def _mix(x, mod, w, tm, b0, b_total, x1_prev=None):
    b, s, d = x.shape
    n_i = s // tm
    hb = tm // HALO
    n_hb = s // HALO
    row = lambda a: a.reshape(1, -1)
    extra_specs, extra_args, aliases = [], [], {}
    if x1_prev is not None:
        extra_specs, extra_args, aliases = [pl.BlockSpec(memory_space=pl.ANY)], [x1_prev], {15: 0}
    return pl.pallas_call(
        functools.partial(_mix_kernel, n_i=n_i),
        grid=(b, n_i),
        in_specs=[
            pl.BlockSpec((1, tm, d), lambda bb, i: (bb, i, 0)),
            pl.BlockSpec((1, HALO, d), lambda bb, i: (bb, jnp.maximum(i * hb - 1, 0), 0)),
            pl.BlockSpec((1, HALO, d), lambda bb, i: (bb, jnp.minimum((i + 1) * hb, n_hb - 1), 0)),
            pl.BlockSpec((1, 6, d), lambda bb, i: (b0 + bb, 0, 0)),
            _const_spec((1, d)),
            _const_spec(w["w_in"].shape),
            _const_spec((3, CONV_W)),
            _const_spec((1, SGU_W)),
            _const_spec((1, SGU_W)),
            _const_spec((HEADS, CHUNK, CHUNK)),
            _const_spec((CHUNK, HEADS)),
            _const_spec((1, CONV_W)),
            _const_spec((1, SGU_W)),
            _const_spec((d, d)),
            _const_spec((1, d)),
        ] + extra_specs,
        out_specs=pl.BlockSpec((1, tm, d), lambda bb, i: (b0 + bb, i, 0)),
        out_shape=jax.ShapeDtypeStruct((b_total, s, d), F32),
        input_output_aliases=aliases,
        scratch_shapes=[pltpu.VMEM((tm + 2 * HALO, CONV_W), F32)],
        compiler_params=pltpu.CompilerParams(
            dimension_semantics=("arbitrary", "arbitrary"), vmem_limit_bytes=VMEM_LIMIT),
        name="token_mix",
    )(x, x, x, mod, row(w["g_pre_mix"]), w["w_in"], w["conv_w"], row(w["ln_v_g"]), row(w["ln_v_b"]),
      w["w_spatial"], w["b_spatial_t"], row(w["g_out_a"]), row(w["g_out_b"]), w["w_out"],
      row(w["g_post_mix"]), *extra_args)


def _router_kernel(x_ref, mod_ref, g_ref, whi_ref, wlo_ref, br_ref,
                   idx_ref, pos_ref, gate_ref, cnt_ref, carry):
    first = jnp.logical_and(pl.program_id(0) == 0, pl.program_id(1) == 0)

    @pl.when(first)
    def _():
        carry[...] = jnp.zeros_like(carry)

    tm = x_ref.shape[1]
    sh2 = mod_ref[0, 3:4, :]
    sc2 = mod_ref[0, 4:5, :]
    h2 = _rms(x_ref[0], g_ref[...]) * (1.0 + sc2) + sh2

    h_hi = h2.astype(BF16)
    h_lo = (h2 - h_hi.astype(F32)).astype(BF16)
    w_hi = whi_ref[...]
    logits_t = _dot(h_hi, w_hi) + (_dot(h_lo, w_hi) + _dot(h_hi, wlo_ref[...]))
    logits = logits_t.T
    scores = jax.nn.sigmoid(logits)
    sel = scores + br_ref[...]

    ri = lax.broadcasted_iota(I32, (N_EXP, LANES), 0)
    e_map = (ri % N_GRP) * GRP_SZ + ri // N_GRP
    gi = lax.broadcasted_iota(I32, (N_GRP, LANES), 0)

    n_chunk = tm // LANES
    sel_chunks, idx_chunks, score_chunks = [], [], []
    for c in range(n_chunk):
        sel_c = sel[:, c * LANES:(c + 1) * LANES]
        sc_c = scores[:, c * LANES:(c + 1) * LANES]
        m1 = sel_c[0:N_GRP]
        m2 = jnp.full_like(m1, -jnp.inf)
        for j in range(1, GRP_SZ):
            vj = sel_c[j * N_GRP:(j + 1) * N_GRP]
            m2 = jnp.maximum(m2, jnp.minimum(m1, vj))
            m1 = jnp.maximum(m1, vj)
        gs = m1 + m2
        grank = jnp.zeros(gs.shape, I32)
        for g2 in range(N_GRP):
            rowv = gs[g2:g2 + 1, :]
            beats = jnp.logical_or(rowv > gs, jnp.logical_and(rowv == gs, g2 < gi))
            grank = grank + beats.astype(I32)
        gmask = grank < TOP_GRP
        m = jnp.concatenate(
            [jnp.where(gmask, sel_c[j * N_GRP:(j + 1) * N_GRP], -jnp.inf) for j in range(GRP_SZ)],
            axis=0)
        chosen = jnp.zeros(m.shape, jnp.bool_)
        idx_rows, score_rows = [], []
        for _ in range(TOP_K):
            mx = jnp.max(m, axis=0, keepdims=True)
            e_min = jnp.min(jnp.where(m == mx, e_map, N_EXP), axis=0, keepdims=True)
            pick = e_map == e_min
            chosen = jnp.logical_or(chosen, pick)
            m = jnp.where(pick, -jnp.inf, m)
            idx_rows.append(e_min)
            score_rows.append(jnp.sum(jnp.where(pick, sc_c, 0.0), axis=0, keepdims=True))
        sel_chunks.append(chosen.astype(F32))
        idx_chunks.append(jnp.concatenate(idx_rows, axis=0))
        score_chunks.append(jnp.concatenate(score_rows, axis=0))
    self32 = jnp.concatenate(sel_chunks, axis=1)
    idx_k = jnp.concatenate(idx_chunks, axis=1)
    score_k = jnp.concatenate(score_chunks, axis=1)
    wsum = jnp.sum(score_k, axis=0, keepdims=True)
    gate_ref[...] = score_k / (wsum + 1e-20) * ROUTED_SCALE
    idx_ref[...] = idx_k

    selb = self32.astype(BF16)
    tr = lax.broadcasted_iota(I32, (tm, tm), 0)
    tc = lax.broadcasted_iota(I32, (tm, tm), 1)
    upper = (tr <= tc).astype(BF16)
    incl = _dot(selb, upper)
    pos_full = incl - self32 + carry[...]
    carry[...] = carry[...] + incl[:, tm - 1:tm]

    pos_chunks = []
    for c in range(n_chunk):
        pos_c = pos_full[:, c * LANES:(c + 1) * LANES]
        rows = []
        for k in range(TOP_K):
            pick = e_map == idx_k[k:k + 1, c * LANES:(c + 1) * LANES]
            rows.append(jnp.sum(jnp.where(pick, pos_c, 0.0), axis=0, keepdims=True))
        pos_chunks.append(jnp.concatenate(rows, axis=0))
    pos_ref[...] = jnp.concatenate(pos_chunks, axis=1).astype(I32)
    cnt_ref[...] = jnp.broadcast_to(carry[...], cnt_ref.shape)


def _router(x1, mod, g_pre_ffn, w_hi, w_lo, br, tm):
    b, s, d = x1.shape
    n_i = s // tm
    t = b * s
    return pl.pallas_call(
        _router_kernel,
        grid=(b, n_i),
        in_specs=[
            pl.BlockSpec((1, tm, d), lambda bb, i: (bb, i, 0)),
            pl.BlockSpec((1, 6, d), lambda bb, i: (bb, 0, 0)),
            _const_spec((1, d)),
            _const_spec((d, N_EXP)),
            _const_spec((d, N_EXP)),
            _const_spec((N_EXP, 1)),
        ],
        out_specs=[
            pl.BlockSpec((TOP_K, tm), lambda bb, i: (0, bb * n_i + i)),
            pl.BlockSpec((TOP_K, tm), lambda bb, i: (0, bb * n_i + i)),
            pl.BlockSpec((TOP_K, tm), lambda bb, i: (0, bb * n_i + i)),
            pl.BlockSpec((N_EXP, LANES), lambda bb, i: (0, 0)),
        ],
        out_shape=[
            jax.ShapeDtypeStruct((TOP_K, t), I32),
            jax.ShapeDtypeStruct((TOP_K, t), I32),
            jax.ShapeDtypeStruct((TOP_K, t), F32),
            jax.ShapeDtypeStruct((N_EXP, LANES), F32),
        ],
        scratch_shapes=[pltpu.VMEM((N_EXP, 1), F32)],
        compiler_params=pltpu.CompilerParams(
            dimension_semantics=("arbitrary", "arbitrary"), vmem_limit_bytes=VMEM_LIMIT),
        name="router",
    )(x1, mod, g_pre_ffn.reshape(1, d), w_hi, w_lo, br)


def _dispatch_kernel(dest_ref, x_ref, mod_ref, g_ref, wg_ref, wu_ref, wd_ref, xs_ref, sh_ref,
                     rowbuf, sem, *, n_b, n_i):
    td = x_ref.shape[1]
    n = pl.program_id(0) * n_i + pl.program_id(1)
    total = n_b * n_i
    n_rows = td * TOP_K * ROW_SUB

    def drain(slot):
        pltpu.make_async_copy(xs_ref.at[pl.ds(0, n_rows)], xs_ref.at[pl.ds(0, n_rows)],
                              sem.at[slot]).wait()

    def step(slot):
        @pl.when(n >= 2)
        def _():
            drain(slot)

        sh2 = mod_ref[0, 3:4, :]
        sc2 = mod_ref[0, 4:5, :]
        h2 = _rms(x_ref[0], g_ref[...]) * (1.0 + sc2) + sh2
        for s in range(ROW_SUB):
            rowbuf[slot, pl.ds(s, td, stride=ROW_SUB), :] = h2[:, s * LANES:(s + 1) * LANES]

        def body(j, c):
            src = rowbuf.at[slot, pl.ds(j * ROW_SUB, ROW_SUB)]
            for k in range(TOP_K):
                dst = dest_ref[0, 0, j * TOP_K + k]
                pltpu.make_async_copy(src, xs_ref.at[pl.ds(dst * ROW_SUB, ROW_SUB)],
                                      sem.at[slot]).start()
            return c
        lax.fori_loop(0, td, body, 0, unroll=2)

        hb = h2.astype(BF16)
        g = _dot(hb, wg_ref[...])
        u = _dot(hb, wu_ref[...])
        sh_ref[0] = _dot((g * jax.nn.sigmoid(g) * u).astype(BF16), wd_ref[...])

        @pl.when(n == total - 1)
        def _():
            drain(slot)
            if total >= 2:
                drain(1 - slot)

    @pl.when(n % 2 == 0)
    def _():
        step(0)

    @pl.when(n % 2 == 1)
    def _():
        step(1)


def _dispatch(dest_flat, x1, mod, w, n_slots, td):
    b, s, d = x1.shape
    n_i = s // td
    dest3 = dest_flat.reshape(b * n_i, 1, td * TOP_K)
    return pl.pallas_call(
        functools.partial(_dispatch_kernel, n_b=b, n_i=n_i),
        grid=(b, n_i),
        in_specs=[pl.BlockSpec((1, 1, td * TOP_K), lambda bb, i: (bb * n_i + i, 0, 0),
                               memory_space=pltpu.SMEM),
                  pl.BlockSpec((1, td, d), lambda bb, i: (bb, i, 0)),
                  pl.BlockSpec((1, 6, d), lambda bb, i: (bb, 0, 0)),
                  _const_spec((1, d)),
                  _const_spec((d, FF)),
                  _const_spec((d, FF)),
                  _const_spec((FF, d))],
        out_specs=[pl.BlockSpec(memory_space=pl.ANY),
                   pl.BlockSpec((1, td, d), lambda bb, i: (bb, i, 0))],
        out_shape=[jax.ShapeDtypeStruct((n_slots * ROW_SUB, LANES), F32),
                   jax.ShapeDtypeStruct((b, s, d), F32)],
        scratch_shapes=[pltpu.VMEM((2, td * ROW_SUB, LANES), F32),
                        pltpu.SemaphoreType.DMA((2,))],
        compiler_params=pltpu.CompilerParams(
            dimension_semantics=("arbitrary", "arbitrary"), vmem_limit_bytes=VMEM_LIMIT),
        name="dispatch",
    )(dest3, x1, mod, w["g_pre_ffn"].reshape(1, d), w["w_sh_gate"], w["w_sh_up"], w["w_sh_down"])


def _expert_kernel(be_ref, nv_ref, nu_ref, xs_ref, wg_ref, wu_ref, wd_ref, ys_ref,
                   wg_s, wu_s, wd_s):
    i = pl.program_id(0)
    tme = xs_ref.shape[0] // ROW_SUB

    @pl.when(i < nu_ref[0])
    def _():
        prev = be_ref[jnp.maximum(i - 1, 0)]
        changed = jnp.logical_or(i == 0, prev != be_ref[i])

        @pl.when(changed)
        def _():
            wg_s[...] = wg_ref[0].astype(BF16)
            wu_s[...] = wu_ref[0].astype(BF16)
            wd_s[...] = wd_ref[0].astype(BF16)

        x = jnp.concatenate([xs_ref[pl.ds(s, tme, stride=ROW_SUB), :] for s in range(ROW_SUB)],
                            axis=1)
        rows = lax.broadcasted_iota(I32, (tme, 1), 0)
        x = jnp.where(rows < nv_ref[i], x, 0.0).astype(BF16)
        g = _dot(x, wg_s[...])
        u = _dot(x, wu_s[...])
        a = (g * jax.nn.sigmoid(g) * u).astype(BF16)
        y = _dot(a, wd_s[...])
        for s in range(ROW_SUB):
            ys_ref[pl.ds(s, tme, stride=ROW_SUB), :] = y[:, s * LANES:(s + 1) * LANES]


def _experts(block_e, nvalid, nused, xs, wg, wu, wd, tme):
    n_blocks = xs.shape[0] // (tme * ROW_SUB)
    blk = lambda i, be, nv, nu: (jnp.minimum(i, nu[0] - 1), 0)
    wsel = lambda i, be, nv, nu: (be[i], 0, 0)
    return pl.pallas_call(
        _expert_kernel,
        grid_spec=pltpu.PrefetchScalarGridSpec(
            num_scalar_prefetch=3,
            grid=(n_blocks,),
            in_specs=[pl.BlockSpec((tme * ROW_SUB, LANES), blk),
                      pl.BlockSpec((1, D_MODEL, FF), wsel),
                      pl.BlockSpec((1, D_MODEL, FF), wsel),
                      pl.BlockSpec((1, FF, D_MODEL), wsel)],
            out_specs=pl.BlockSpec((tme * ROW_SUB, LANES), blk),
            scratch_shapes=[pltpu.VMEM((D_MODEL, FF), BF16),
                            pltpu.VMEM((D_MODEL, FF), BF16),
                            pltpu.VMEM((FF, D_MODEL), BF16)],
        ),
        out_shape=jax.ShapeDtypeStruct(xs.shape, F32),
        compiler_params=pltpu.CompilerParams(
            dimension_semantics=("arbitrary",), vmem_limit_bytes=VMEM_LIMIT),
        name="experts",
    )(block_e, nvalid, nused, xs, wg, wu, wd)


N_GATHER_BUF = 3


def _combine_kernel(dest_ref, dest1_ref, dest2_ref, gexp_ref, x_ref, sh_ref, mod_ref, gpost_ref,
                    ys_ref, o_ref, buf, rsum, sem, *, n_b, n_i):
    tc = x_ref.shape[1]
    n = pl.program_id(0) * n_i + pl.program_id(1)
    total = n_b * n_i
    n_rows = tc * TOP_K * ROW_SUB
    half = ROW_SUB // 2

    def gather_rows(idx_ref, slot, unroll):
        def body(j, c):
            for k in range(TOP_K):
                src = idx_ref[0, 0, j * TOP_K + k]
                pltpu.make_async_copy(ys_ref.at[pl.ds(src * ROW_SUB, ROW_SUB)],
                                      buf.at[slot, pl.ds((j * TOP_K + k) * ROW_SUB, ROW_SUB)],
                                      sem.at[slot]).start()
            return c
        lax.fori_loop(0, tc, body, 0, unroll=unroll)

    def drain(slot):
        pltpu.make_async_copy(ys_ref.at[pl.ds(0, n_rows)], buf.at[slot], sem.at[slot]).wait()

    def step(slot):
        slot1 = (slot + 1) % N_GATHER_BUF
        slot2 = (slot + 2) % N_GATHER_BUF

        @pl.when(n == 0)
        def _():
            gather_rows(dest_ref, slot, 2)
            gather_rows(dest1_ref, slot1, 2)

        drain(slot)
        gather_rows(dest2_ref, slot2, True)

        for j in range(tc):
            base = j * (TOP_K * ROW_SUB)
            gk = jnp.broadcast_to(gexp_ref[pl.ds(j * TOP_K, 1), :], (half, LANES))
            acc0 = gk * buf[slot, pl.ds(base, half), :]
            acc1 = gk * buf[slot, pl.ds(base + half, half), :]
            for k in range(1, TOP_K):
                gk = jnp.broadcast_to(gexp_ref[pl.ds(j * TOP_K + k, 1), :], (half, LANES))
                acc0 = acc0 + gk * buf[slot, pl.ds(base + k * ROW_SUB, half), :]
                acc1 = acc1 + gk * buf[slot, pl.ds(base + k * ROW_SUB + half, half), :]
            rsum[pl.ds(j * ROW_SUB, half), :] = acc0
            rsum[pl.ds(j * ROW_SUB + half, half), :] = acc1

        routed = jnp.concatenate([rsum[pl.ds(s, tc, stride=ROW_SUB), :] for s in range(ROW_SUB)],
                                 axis=1)
        ffn = routed + sh_ref[0]
        gt2 = mod_ref[0, 5:6, :]
        o_ref[0] = x_ref[0] + gt2 * _rms(ffn, gpost_ref[...])

        @pl.when(n + 1 == total)
        def _():
            drain(slot1)
            drain(slot2)

    for slot in range(N_GATHER_BUF):
        pl.when(n % N_GATHER_BUF == slot)(functools.partial(step, slot))


def _combine(dest_flat, gates_rows,b0, b, x1, shared, mod, w, ys, tc):
    _, s, d = x1.shape
    n_i = s // tc
    t_all = dest_flat.shape[0] // TOP_K
    dest3 = dest_flat.reshape(t_all // tc, 1, tc * TOP_K)
    t0 = b0 * n_i
    last = t0 + b * n_i - 1
    cur = lambda bb, i: (t0 + bb * n_i + i, 0, 0)
    nxt1 = lambda bb, i: (jnp.minimum(t0 + bb * n_i + i + 1, last), 0, 0)
    nxt2 = lambda bb, i: (jnp.minimum(t0 + bb * n_i + i + 2, last), 0, 0)
    return pl.pallas_call(
        functools.partial(_combine_kernel, n_b=b, n_i=n_i),
        grid=(b, n_i),
        in_specs=[
            pl.BlockSpec((1, 1, tc * TOP_K), cur, memory_space=pltpu.SMEM),
            pl.BlockSpec((1, 1, tc * TOP_K), nxt1, memory_space=pltpu.SMEM),
            pl.BlockSpec((1, 1, tc * TOP_K), nxt2, memory_space=pltpu.SMEM),
            pl.BlockSpec((tc * TOP_K, LANES), lambda bb, i: (t0 + bb * n_i + i, 0)),
            pl.BlockSpec((1, tc, d), lambda bb, i: (b0 + bb, i, 0)),
            pl.BlockSpec((1, tc, d), lambda bb, i: (b0 + bb, i, 0)),
            pl.BlockSpec((1, 6, d), lambda bb, i: (b0 + bb, 0, 0)),
            _const_spec((1, d)),
            pl.BlockSpec(memory_space=pl.ANY),
        ],
        out_specs=pl.BlockSpec((1, tc, d), lambda bb, i: (bb, i, 0)),
        out_shape=jax.ShapeDtypeStruct((b, s, d), F32),
        scratch_shapes=[pltpu.VMEM((N_GATHER_BUF, tc * TOP_K * ROW_SUB, LANES), F32),
                        pltpu.VMEM((tc * ROW_SUB, LANES), F32),
                        pltpu.SemaphoreType.DMA((N_GATHER_BUF,))],
        compiler_params=pltpu.CompilerParams(
            dimension_semantics=("arbitrary", "arbitrary"), vmem_limit_bytes=VMEM_LIMIT),
        name="combine",
    )(dest3, dest3, dest3, gates_rows, x1, shared, mod, w["g_post_ffn"].reshape(1, d), ys)


def _tile_sizes(s):
    return dict(tm_mix=min(256, s), tm_route=min(512, s), td=min(256, s), tme=min(512, s),
                tc=min(128, s))


def _layer(x_prompt, x_sample, c_prompt, c_sample, lw):
    bp, s, d = x_prompt.shape
    bs = x_sample.shape[0]
    ts = _tile_sizes(s)

    nb = bp + bs
    nb_pad = -(-nb // 8) * 8
    c_all = jnp.concatenate([c_prompt, c_sample, jnp.zeros((nb_pad - nb, d), F32)], axis=0)
    mod = _ada(c_all, lw["w_ada"], lw["b_ada"])[:nb].reshape(nb, 6, d)

    w = dict(lw)
    w["w_in"] = lw["w_in"].astype(BF16)
    w["w_out"] = lw["w_out"].astype(BF16)
    w["w_spatial"] = lw["w_spatial"].astype(BF16)
    w["b_spatial_t"] = lw["b_spatial"].T
    w["w_sh_gate"] = lw["w_sh_gate"].astype(BF16)
    w["w_sh_up"] = lw["w_sh_up"].astype(BF16)
    w["w_sh_down"] = lw["w_sh_down"].astype(BF16)

    x1 = _mix(x_prompt, mod, w, ts["tm_mix"], 0, nb)
    x1 = _mix(x_sample, mod, w, ts["tm_mix"], bp, nb, x1_prev=x1)

    perm = (jnp.arange(N_EXP) % N_GRP) * GRP_SZ + jnp.arange(N_EXP) // N_GRP
    wr = lw["w_router"][:, perm]
    wr_hi = wr.astype(BF16)
    wr_lo = (wr - wr_hi.astype(F32)).astype(BF16)
    br = lw["router_bias"][perm].reshape(N_EXP, 1)
    idx, pos, gates, cnt_all = _router(x1, mod, lw["g_pre_ffn"], wr_hi, wr_lo, br, ts["tm_route"])

    tme = ts["tme"]
    t_all = nb * s
    counts = cnt_all[:, 0].astype(I32).reshape(GRP_SZ, N_GRP).T.reshape(N_EXP)
    padded = (counts + tme - 1) // tme * tme
    pad_end = jnp.cumsum(padded)
    off = pad_end - padded
    gates_rows = jnp.broadcast_to(gates.T.reshape(t_all * TOP_K, 1), (t_all * TOP_K, LANES))
    e_ids = jnp.arange(N_EXP, dtype=I32)
    off_of_idx = jnp.sum(jnp.where(idx[:, :, None] == e_ids, off, 0), axis=-1)
    dest_flat = (off_of_idx + pos).T.reshape(t_all * TOP_K)
    n_blocks = -(-t_all * TOP_K // tme) + N_EXP
    nused = (pad_end[-1] // tme).astype(I32)
    blk_ids = jnp.minimum(jnp.arange(n_blocks, dtype=I32), nused - 1)
    blk_start = blk_ids * tme
    block_e = jnp.minimum(jnp.sum((pad_end[None, :] <= blk_start[:, None]).astype(I32), axis=1),
                          N_EXP - 1)
    be_hot = block_e[:, None] == e_ids
    cnt_of_blk = jnp.sum(jnp.where(be_hot, counts, 0), axis=1)
    off_of_blk = jnp.sum(jnp.where(be_hot, off, 0), axis=1)
    nvalid = jnp.clip(cnt_of_blk - (blk_start - off_of_blk), 0, tme).astype(I32)

    xs, shared = _dispatch(dest_flat, x1, mod, w, n_blocks * tme, ts["td"])
    ys = _experts(block_e, nvalid, nused.reshape(1), xs, lw["w_exp_gate"], lw["w_exp_up"],
                  lw["w_exp_down"], tme)

    y_p = _combine(dest_flat, gates_rows,0, bp, x1, shared, mod, w, ys, ts["tc"])
    y_s = _combine(dest_flat, gates_rows,bp, bs, x1, shared, mod, w, ys, ts["tc"])
    return y_p, y_s


def kernel(x_prompt, x_sample, c_prompt, c_sample, w_ada, b_ada, g_pre_mix, w_in, conv_w, ln_v_g, ln_v_b, w_spatial, b_spatial, g_out_a, g_out_b, w_out, g_post_mix, g_pre_ffn, w_router, router_bias, w_exp_gate, w_exp_up, w_exp_down, w_sh_gate, w_sh_up, w_sh_down, g_post_ffn):
    names = ("w_ada", "b_ada", "g_pre_mix", "w_in", "conv_w", "ln_v_g", "ln_v_b", "w_spatial",
             "b_spatial", "g_out_a", "g_out_b", "w_out", "g_post_mix", "g_pre_ffn", "w_router",
             "router_bias", "w_exp_gate", "w_exp_up", "w_exp_down", "w_sh_gate", "w_sh_up",
             "w_sh_down", "g_post_ffn")
    stacked = (w_ada, b_ada, g_pre_mix, w_in, conv_w, ln_v_g, ln_v_b, w_spatial, b_spatial, g_out_a,
               g_out_b, w_out, g_post_mix, g_pre_ffn, w_router, router_bias, w_exp_gate, w_exp_up,
               w_exp_down, w_sh_gate, w_sh_up, w_sh_down, g_post_ffn)
    y_p, y_s = x_prompt, x_sample
    for l in range(w_ada.shape[0]):
        lw = {n: a[l] for n, a in zip(names, stacked)}
        y_p, y_s = _layer(y_p, y_s, c_prompt, c_sample, lw)
    return (y_p, y_s)
```

```python
import functools

import jax
import jax.numpy as jnp
from jax import lax
from jax.experimental import pallas as pl
from jax.experimental.pallas import tpu as pltpu
from jax.experimental.pallas import tpu_sc as plsc

F32 = jnp.float32
BF16 = jnp.bfloat16
I32 = jnp.int32

D_MODEL = 2048
CONV_W = 1024
SGU_W = 1024
HEADS = 8
HEAD_DIM = 128
CHUNK = 128
N_EXP = 64
TOP_K = 8
N_GRP = 8
GRP_SZ = 8
TOP_GRP = 4
FF = 512
EPS = 1e-6
ROUTED_SCALE = 2.5
LANES = 128
ROW_SUB = D_MODEL // LANES
HALO = 8
EXPERT_SUB_BLOCKS = 4

VMEM_LIMIT = 56 * 1024 * 1024


def _rms(x, g):
    return x * lax.rsqrt(jnp.mean(x * x, axis=-1, keepdims=True) + EPS) * g


def _dot(a, b):
    return jnp.dot(a, b, preferred_element_type=F32)


def _const_spec(shape):
    nd = len(shape)
    return pl.BlockSpec(shape, lambda *_: (0,) * nd, pipeline_mode=pl.Buffered(1))


def _ada_kernel(c_ref, w_ref, b_ref, o_ref):
    c = c_ref[...]
    s = c * jax.nn.sigmoid(c)
    o_ref[...] = jnp.dot(s, w_ref[...], precision=lax.Precision.HIGHEST,
                         preferred_element_type=F32) + b_ref[...]


def _ada(c_pad, w_ada, b_ada):
    n = w_ada.shape[1]
    bn = 1536
    return pl.pallas_call(
        _ada_kernel,
        grid=(n // bn,),
        in_specs=[pl.BlockSpec(c_pad.shape, lambda j: (0, 0)),
                  pl.BlockSpec((D_MODEL, bn), lambda j: (0, j)),
                  pl.BlockSpec((1, bn), lambda j: (0, j))],
        out_specs=pl.BlockSpec((c_pad.shape[0], bn), lambda j: (0, j)),
        out_shape=jax.ShapeDtypeStruct((c_pad.shape[0], n), F32),
        compiler_params=pltpu.CompilerParams(vmem_limit_bytes=VMEM_LIMIT),
        name="ada_mod",
    )(c_pad, w_ada, b_ada.reshape(1, n))


def _mix_kernel(x_ref, xp_ref, xn_ref, mod_ref, gpre_ref, win_ref, cw_ref, lng_ref, lnb_ref,
                wsp_ref, bsp_ref, ga_ref, gb_ref, wout_ref, gpost_ref, *rest, n_i):
    o_ref, zbuf = rest[-2:]
    i = pl.program_id(1)
    tm = x_ref.shape[1]
    sh1 = mod_ref[0, 0:1, :]
    sc1 = mod_ref[0, 1:2, :]
    gt1 = mod_ref[0, 2:3, :]
    gpre = gpre_ref[...]

    def prenorm(xv):
        return _rms(xv, gpre) * (1.0 + sc1) + sh1

    x = x_ref[0]
    h = jnp.concatenate([prenorm(x), prenorm(xp_ref[0]), prenorm(xn_ref[0])], axis=0).astype(BF16)
    ht = h[:tm]

    cz = _dot(h, win_ref[:, CONV_W:3 * CONV_W])
    z = cz[:, :CONV_W] * cz[:, CONV_W:]
    zbuf[HALO:HALO + tm, :] = z[:tm]
    zbuf[0:HALO, :] = jnp.where(i == 0, 0.0, z[tm:tm + HALO])
    zbuf[HALO + tm:, :] = jnp.where(i == n_i - 1, 0.0, z[tm + HALO:])
    cw = cw_ref[...]
    conv = (zbuf[HALO - 1:HALO - 1 + tm, :] * cw[0:1] + zbuf[HALO:HALO + tm, :] * cw[1:2]
            + zbuf[HALO + 1:HALO + 1 + tm, :] * cw[2:3])
    bg = _dot(ht, win_ref[:, :CONV_W])
    ya = _rms(bg * conv, ga_ref[...])

    uv = _dot(ht, win_ref[:, 3 * CONV_W:])
    u = jax.nn.gelu(uv[:, :SGU_W], approximate=True)
    v = jax.nn.gelu(uv[:, SGU_W:], approximate=True)
    lng = lng_ref[...]
    lnb = lnb_ref[...]
    nchunk = tm // CHUNK
    parts = []
    for hd in range(HEADS):
        lo, hi = hd * HEAD_DIM, (hd + 1) * HEAD_DIM
        vh = v[:, lo:hi]
        mu = jnp.mean(vh, axis=-1, keepdims=True)
        vc = vh - mu
        var = jnp.mean(vc * vc, axis=-1, keepdims=True)
        vn = vc * lax.rsqrt(var + EPS) * lng[:, lo:hi] + lnb[:, lo:hi]
        vcat = jnp.concatenate([vn[c * CHUNK:(c + 1) * CHUNK] for c in range(nchunk)],
                               axis=1).astype(BF16)
        sp = _dot(wsp_ref[hd], vcat) + bsp_ref[:, hd:hd + 1]
        sp_rows = jnp.concatenate([sp[:, c * HEAD_DIM:(c + 1) * HEAD_DIM] for c in range(nchunk)],
                                  axis=0)
        parts.append(u[:, lo:hi] * sp_rows)
    yb = _rms(jnp.concatenate(parts, axis=1), gb_ref[...])

    cat = jnp.concatenate([ya, yb], axis=1).astype(BF16)
    mix = _dot(cat, wout_ref[...])
    o_ref[0] = x + gt1 * _rms(mix, gpost_ref[...])


def _mix(x, mod, w, tm, b0, b_total, x1_prev=None):
    b, s, d = x.shape
    n_i = s // tm
    hb = tm // HALO
    n_hb = s // HALO
    row = lambda a: a.reshape(1, -1)
    extra_specs, extra_args, aliases = [], [], {}
    if x1_prev is not None:
        extra_specs, extra_args, aliases = [pl.BlockSpec(memory_space=pl.ANY)], [x1_prev], {15: 0}
    return pl.pallas_call(
        functools.partial(_mix_kernel, n_i=n_i),
        grid=(b, n_i),
        in_specs=[
            pl.BlockSpec((1, tm, d), lambda bb, i: (bb, i, 0)),
            pl.BlockSpec((1, HALO, d), lambda bb, i: (bb, jnp.maximum(i * hb - 1, 0), 0)),
            pl.BlockSpec((1, HALO, d), lambda bb, i: (bb, jnp.minimum((i + 1) * hb, n_hb - 1), 0)),
            pl.BlockSpec((1, 6, d), lambda bb, i: (b0 + bb, 0, 0)),
            _const_spec((1, d)),
            _const_spec(w["w_in"].shape),
            _const_spec((3, CONV_W)),
            _const_spec((1, SGU_W)),
            _const_spec((1, SGU_W)),
            _const_spec((HEADS, CHUNK, CHUNK)),
            _const_spec((CHUNK, HEADS)),
            _const_spec((1, CONV_W)),
            _const_spec((1, SGU_W)),
            _const_spec((d, d)),
            _const_spec((1, d)),
        ] + extra_specs,
        out_specs=pl.BlockSpec((1, tm, d), lambda bb, i: (b0 + bb, i, 0)),
        out_shape=jax.ShapeDtypeStruct((b_total, s, d), F32),
        input_output_aliases=aliases,
        scratch_shapes=[pltpu.VMEM((tm + 2 * HALO, CONV_W), F32)],
        compiler_params=pltpu.CompilerParams(
            dimension_semantics=("arbitrary", "arbitrary"), vmem_limit_bytes=VMEM_LIMIT),
        name="token_mix",
    )(x, x, x, mod, row(w["g_pre_mix"]), w["w_in"], w["conv_w"], row(w["ln_v_g"]), row(w["ln_v_b"]),
      w["w_spatial"], w["b_spatial_t"], row(w["g_out_a"]), row(w["g_out_b"]), w["w_out"],
      row(w["g_post_mix"]), *extra_args)


def _router_kernel(x_ref, mod_ref, g_ref, whi_ref, wlo_ref, br_ref,
                   h2_ref, idx_ref, pos_ref, gate_ref, cnt_ref, carry):
    first = jnp.logical_and(pl.program_id(0) == 0, pl.program_id(1) == 0)

    @pl.when(first)
    def _():
        carry[...] = jnp.zeros_like(carry)

    tm = x_ref.shape[1]
    sh2 = mod_ref[0, 3:4, :]
    sc2 = mod_ref[0, 4:5, :]
    h2 = _rms(x_ref[0], g_ref[...]) * (1.0 + sc2) + sh2
    for s in range(ROW_SUB):
        h2_ref[pl.ds(s, tm, stride=ROW_SUB), :] = h2[:, s * LANES:(s + 1) * LANES]

    h_hi = h2.astype(BF16)

    h_lo = (h2 - h_hi.astype(F32)).astype(BF16)
    w_hi = whi_ref[...]
    logits_t = _dot(h_hi, w_hi) + (_dot(h_lo, w_hi) + _dot(h_hi, wlo_ref[...]))
    logits = logits_t.T
    scores = jax.nn.sigmoid(logits)
    sel = scores + br_ref[...]

    ri = lax.broadcasted_iota(I32, (N_EXP, LANES), 0)
    e_map = (ri % N_GRP) * GRP_SZ + ri // N_GRP
    gi = lax.broadcasted_iota(I32, (N_GRP, LANES), 0)

    n_chunk = tm // LANES
    sel_chunks, idx_chunks, score_chunks = [], [], []
    for c in range(n_chunk):
        sel_c = sel[:, c * LANES:(c + 1) * LANES]
        sc_c = scores[:, c * LANES:(c + 1) * LANES]
        m1 = sel_c[0:N_GRP]
        m2 = jnp.full_like(m1, -jnp.inf)
        for j in range(1, GRP_SZ):
            vj = sel_c[j * N_GRP:(j + 1) * N_GRP]
            m2 = jnp.maximum(m2, jnp.minimum(m1, vj))
            m1 = jnp.maximum(m1, vj)
        gs = m1 + m2
        grank = jnp.zeros(gs.shape, I32)
        for g2 in range(N_GRP):
            rowv = gs[g2:g2 + 1, :]
            beats = jnp.logical_or(rowv > gs, jnp.logical_and(rowv == gs, g2 < gi))
            grank = grank + beats.astype(I32)
        gmask = grank < TOP_GRP
        m = jnp.concatenate(
            [jnp.where(gmask, sel_c[j * N_GRP:(j + 1) * N_GRP], -jnp.inf) for j in range(GRP_SZ)],
            axis=0)
        chosen = jnp.zeros(m.shape, jnp.bool_)
        idx_rows, score_rows = [], []
        for _ in range(TOP_K):
            mx = jnp.max(m, axis=0, keepdims=True)
            e_min = jnp.min(jnp.where(m == mx, e_map, N_EXP), axis=0, keepdims=True)
            pick = e_map == e_min
            chosen = jnp.logical_or(chosen, pick)
            m = jnp.where(pick, -jnp.inf, m)
            idx_rows.append(e_min)
            score_rows.append(jnp.sum(jnp.where(pick, sc_c, 0.0), axis=0, keepdims=True))
        sel_chunks.append(chosen.astype(F32))
        idx_chunks.append(jnp.concatenate(idx_rows, axis=0))
        score_chunks.append(jnp.concatenate(score_rows, axis=0))
    self32 = jnp.concatenate(sel_chunks, axis=1)
    idx_k = jnp.concatenate(idx_chunks, axis=1)
    score_k = jnp.concatenate(score_chunks, axis=1)
    wsum = jnp.sum(score_k, axis=0, keepdims=True)
    gate_ref[...] = score_k / (wsum + 1e-20) * ROUTED_SCALE
    idx_ref[...] = idx_k

    selb = self32.astype(BF16)
    tr = lax.broadcasted_iota(I32, (tm, tm), 0)
    tc = lax.broadcasted_iota(I32, (tm, tm), 1)
    upper = (tr <= tc).astype(BF16)
    incl = _dot(selb, upper)
    pos_full = incl - self32 + carry[...]
    carry[...] = carry[...] + incl[:, tm - 1:tm]

    pos_chunks = []
    for c in range(n_chunk):
        pos_c = pos_full[:, c * LANES:(c + 1) * LANES]
        rows = []
        for k in range(TOP_K):
            pick = e_map == idx_k[k:k + 1, c * LANES:(c + 1) * LANES]
            rows.append(jnp.sum(jnp.where(pick, pos_c, 0.0), axis=0, keepdims=True))
        pos_chunks.append(jnp.concatenate(rows, axis=0))
    pos_ref[...] = jnp.concatenate(pos_chunks, axis=1).astype(I32)
    cnt_ref[...] = jnp.broadcast_to(carry[...], cnt_ref.shape)


def _router(x1, mod, w, w_hi, w_lo, br, tm):
    b, s, d = x1.shape
    n_i = s // tm
    t = b * s
    return pl.pallas_call(
        _router_kernel,
        grid=(b, n_i),
        in_specs=[
            pl.BlockSpec((1, tm, d), lambda bb, i: (bb, i, 0)),
            pl.BlockSpec((1, 6, d), lambda bb, i: (bb, 0, 0)),
            _const_spec((1, d)),
            _const_spec((d, N_EXP)),
            _const_spec((d, N_EXP)),
            _const_spec((N_EXP, 1)),
        ],
        out_specs=[
            pl.BlockSpec((tm * ROW_SUB, LANES), lambda bb, i: (bb * n_i + i, 0)),
            pl.BlockSpec((TOP_K, tm), lambda bb, i: (0, bb * n_i + i)),
            pl.BlockSpec((TOP_K, tm), lambda bb, i: (0, bb * n_i + i)),
            pl.BlockSpec((TOP_K, tm), lambda bb, i: (0, bb * n_i + i)),
            pl.BlockSpec((N_EXP, LANES), lambda bb, i: (0, 0)),
        ],
        out_shape=[
            jax.ShapeDtypeStruct((t * ROW_SUB, LANES), F32),
            jax.ShapeDtypeStruct((TOP_K, t), I32),
            jax.ShapeDtypeStruct((TOP_K, t), I32),
            jax.ShapeDtypeStruct((TOP_K, t), F32),
            jax.ShapeDtypeStruct((N_EXP, LANES), F32),
        ],
        scratch_shapes=[pltpu.VMEM((N_EXP, 1), F32)],
        compiler_params=pltpu.CompilerParams(
            dimension_semantics=("arbitrary", "arbitrary"), vmem_limit_bytes=VMEM_LIMIT),
        name="router",
    )(x1, mod, w["g_pre_ffn"].reshape(1, d), w_hi, w_lo, br)


def _dispatch_kernel(dest_ref, x_ref, mod_ref, g_ref, wg_ref, wu_ref, wd_ref, xs_ref, sh_ref,
                     rowbuf, sem, *, n_b, n_i):
    td = x_ref.shape[1]
    n = pl.program_id(0) * n_i + pl.program_id(1)
    total = n_b * n_i
    n_rows = td * TOP_K * ROW_SUB

    def drain(slot):
        pltpu.make_async_copy(xs_ref.at[pl.ds(0, n_rows)], xs_ref.at[pl.ds(0, n_rows)],
                              sem.at[slot]).wait()

    def step(slot):
        @pl.when(n >= 2)
        def _():
            drain(slot)

        sh2 = mod_ref[0, 3:4, :]
        sc2 = mod_ref[0, 4:5, :]
        h2 = _rms(x_ref[0], g_ref[...]) * (1.0 + sc2) + sh2
        for s in range(ROW_SUB):
            rowbuf[slot, pl.ds(s, td, stride=ROW_SUB), :] = h2[:, s * LANES:(s + 1) * LANES]

        def body(j, c):
            src = rowbuf.at[slot, pl.ds(j * ROW_SUB, ROW_SUB)]
            for k in range(TOP_K):
                dst = dest_ref[0, 0, j * TOP_K + k]
                pltpu.make_async_copy(src, xs_ref.at[pl.ds(dst * ROW_SUB, ROW_SUB)],
                                      sem.at[slot]).start()
            return c
        lax.fori_loop(0, td, body, 0, unroll=2)

        hb = h2.astype(BF16)
        g = _dot(hb, wg_ref[...])
        u = _dot(hb, wu_ref[...])
        sh_ref[0] = _dot((g * jax.nn.sigmoid(g) * u).astype(BF16), wd_ref[...])

        @pl.when(n == total - 1)
        def _():
            drain(slot)
            if total >= 2:
                drain(1 - slot)

    @pl.when(n % 2 == 0)
    def _():
        step(0)

    @pl.when(n % 2 == 1)
    def _():
        step(1)


def _dispatch(dest_flat, x1, mod, w, n_slots, td):
    b, s, d = x1.shape
    n_i = s // td
    dest3 = dest_flat.reshape(b * n_i, 1, td * TOP_K)
    return pl.pallas_call(
        functools.partial(_dispatch_kernel, n_b=b, n_i=n_i),
        grid=(b, n_i),
        in_specs=[pl.BlockSpec((1, 1, td * TOP_K), lambda bb, i: (bb * n_i + i, 0, 0),
                               memory_space=pltpu.SMEM),
                  pl.BlockSpec((1, td, d), lambda bb, i: (bb, i, 0)),
                  pl.BlockSpec((1, 6, d), lambda bb, i: (bb, 0, 0)),
                  _const_spec((1, d)),
                  _const_spec((d, FF)),
                  _const_spec((d, FF)),
                  _const_spec((FF, d))],
        out_specs=[pl.BlockSpec(memory_space=pl.ANY),
                   pl.BlockSpec((1, td, d), lambda bb, i: (bb, i, 0))],
        out_shape=[jax.ShapeDtypeStruct((n_slots * ROW_SUB, LANES), F32),
                   jax.ShapeDtypeStruct((b, s, d), F32)],
        scratch_shapes=[pltpu.VMEM((2, td * ROW_SUB, LANES), F32),
                        pltpu.SemaphoreType.DMA((2,))],
        compiler_params=pltpu.CompilerParams(
            dimension_semantics=("arbitrary", "arbitrary"), vmem_limit_bytes=VMEM_LIMIT),
        name="dispatch",
    )(dest3, x1, mod, w["g_pre_ffn"].reshape(1, d), w["w_sh_gate"], w["w_sh_up"], w["w_sh_down"])


def _invert_slots(dest_flat, n_slots, n_spare):
    assert n_spare & (n_spare - 1) == 0
    info = plsc.get_sparse_core_info()
    n_workers = info.num_cores * info.num_subcores
    lanes = info.num_lanes
    n_pairs = dest_flat.shape[0]
    per_worker = n_slots // n_workers
    chunk = min(n_pairs, 16384)
    assert n_slots % (n_workers * lanes) == 0 and n_pairs % chunk == 0 and chunk % lanes == 0
    mesh = plsc.VectorSubcoreMesh(core_axis_name="core", subcore_axis_name="subcore")

    def body(dest_hbm, inv_hbm, dbuf, obuf):
        wid = lax.axis_index("subcore") * info.num_cores + lax.axis_index("core")
        lo = wid * per_worker

        @pl.loop(0, per_worker, step=lanes)
        def _(r):
            slot = lo + r + lax.iota(I32, lanes)
            obuf[pl.ds(r, lanes)] = n_pairs + jnp.bitwise_and(slot, n_spare - 1)

        @pl.loop(0, n_pairs, step=chunk)
        def _(c0):
            pltpu.sync_copy(dest_hbm.at[pl.ds(c0, chunk)], dbuf)

            @pl.loop(0, chunk, step=lanes)
            def _(j):
                loc = dbuf[pl.ds(j, lanes)] - lo
                mine = jnp.logical_and(loc >= 0, loc < per_worker)
                pair = c0 + j + lax.iota(I32, lanes)
                plsc.store_scatter(obuf, [jnp.where(mine, loc, 0)], pair, mask=mine)

        pltpu.sync_copy(obuf, inv_hbm.at[pl.ds(lo, per_worker)])

    return pl.kernel(
        body, out_type=jax.ShapeDtypeStruct((n_slots,), I32), mesh=mesh,
        scratch_types=[pltpu.VMEM((chunk,), I32), pltpu.VMEM((per_worker,), I32)],
        compiler_params=pltpu.CompilerParams(needs_layout_passes=False),
        name="invert_slots",
    )(dest_flat)


def _expert_kernel(be_ref, nv_ref, nu_ref, xs_ref, wg_ref, wu_ref, wd_ref, ys_ref,
                   wg_s, wu_s, wd_s):
    i = pl.program_id(0)
    tme = xs_ref.shape[0] // ROW_SUB

    @pl.when(i < nu_ref[0])
    def _():
        prev = be_ref[jnp.maximum(i - 1, 0)]
        changed = jnp.logical_or(i == 0, prev != be_ref[i])

        @pl.when(changed)
        def _():
            wg_s[...] = wg_ref[0].astype(BF16)
            wu_s[...] = wu_ref[0].astype(BF16)
            wd_s[...] = wd_ref[0].astype(BF16)

        sub = tme // EXPERT_SUB_BLOCKS
        for q in range(EXPERT_SUB_BLOCKS):
            r0 = q * sub
            x = jnp.concatenate(
                [xs_ref[pl.ds(r0 * ROW_SUB + s, sub, stride=ROW_SUB), :] for s in range(ROW_SUB)],
                axis=1)
            rows = r0 + lax.broadcasted_iota(I32, (sub, 1), 0)
            x = jnp.where(rows < nv_ref[i], x, 0.0).astype(BF16)
            g = _dot(x, wg_s[...])
            u = _dot(x, wu_s[...])
            a = (g * jax.nn.sigmoid(g) * u).astype(BF16)
            y = _dot(a, wd_s[...])
            for s in range(ROW_SUB):
                ys_ref[pl.ds(r0 * ROW_SUB + s, sub, stride=ROW_SUB), :] = y[:, s * LANES:(s + 1) * LANES]


def _experts(block_e, nvalid, nused, xs, wg, wu, wd, tme):
    n_blocks = xs.shape[0] // (tme * ROW_SUB)
    blk = lambda i, be, nv, nu: (jnp.minimum(i, nu[0] - 1), 0)
    wsel = lambda i, be, nv, nu: (be[i], 0, 0)
    return pl.pallas_call(
        _expert_kernel,
        grid_spec=pltpu.PrefetchScalarGridSpec(
            num_scalar_prefetch=3,
            grid=(n_blocks,),
            in_specs=[pl.BlockSpec((tme * ROW_SUB, LANES), blk),
                      pl.BlockSpec((1, D_MODEL, FF), wsel),
                      pl.BlockSpec((1, D_MODEL, FF), wsel),
                      pl.BlockSpec((1, FF, D_MODEL), wsel)],
            out_specs=pl.BlockSpec((tme * ROW_SUB, LANES), blk),
            scratch_shapes=[pltpu.VMEM((D_MODEL, FF), BF16),
                            pltpu.VMEM((D_MODEL, FF), BF16),
                            pltpu.VMEM((FF, D_MODEL), BF16)],
        ),
        out_shape=jax.ShapeDtypeStruct(xs.shape, F32),
        compiler_params=pltpu.CompilerParams(
            dimension_semantics=("arbitrary",), vmem_limit_bytes=VMEM_LIMIT),
        name="experts",
    )(block_e, nvalid, nused, xs, wg, wu, wd)


N_GATHER_BUF = 3


def _combine_kernel(dest_ref, dest1_ref, dest2_ref, gexp_ref, x_ref, sh_ref, mod_ref, gpost_ref,
                    ys_ref, o_ref, buf, rsum, sem, *, n_b, n_i):
    tc = x_ref.shape[1]
    n = pl.program_id(0) * n_i + pl.program_id(1)
    total = n_b * n_i
    n_rows = tc * TOP_K * ROW_SUB
    half = ROW_SUB // 2

    def gather_rows(idx_ref, slot, unroll):
        def body(j, c):
            for k in range(TOP_K):
                src = idx_ref[0, 0, j * TOP_K + k]
                pltpu.make_async_copy(ys_ref.at[pl.ds(src * ROW_SUB, ROW_SUB)],
                                      buf.at[slot, pl.ds((j * TOP_K + k) * ROW_SUB, ROW_SUB)],
                                      sem.at[slot]).start()
            return c
        lax.fori_loop(0, tc, body, 0, unroll=unroll)

    def drain(slot):
        pltpu.make_async_copy(ys_ref.at[pl.ds(0, n_rows)], buf.at[slot], sem.at[slot]).wait()

    def step(slot):
        slot1 = (slot + 1) % N_GATHER_BUF
        slot2 = (slot + 2) % N_GATHER_BUF

        @pl.when(n == 0)
        def _():
            gather_rows(dest_ref, slot, 2)
            gather_rows(dest1_ref, slot1, 2)

        drain(slot)
        gather_rows(dest2_ref, slot2, True)

        for j in range(tc):
            base = j * (TOP_K * ROW_SUB)
            gk = jnp.broadcast_to(gexp_ref[pl.ds(j * TOP_K, 1), :], (half, LANES))
            acc0 = gk * buf[slot, pl.ds(base, half), :]
            acc1 = gk * buf[slot, pl.ds(base + half, half), :]
            for k in range(1, TOP_K):
                gk = jnp.broadcast_to(gexp_ref[pl.ds(j * TOP_K + k, 1), :], (half, LANES))
                acc0 = acc0 + gk * buf[slot, pl.ds(base + k * ROW_SUB, half), :]
                acc1 = acc1 + gk * buf[slot, pl.ds(base + k * ROW_SUB + half, half), :]
            rsum[pl.ds(j * ROW_SUB, half), :] = acc0
            rsum[pl.ds(j * ROW_SUB + half, half), :] = acc1

        routed = jnp.concatenate([rsum[pl.ds(s, tc, stride=ROW_SUB), :] for s in range(ROW_SUB)],
                                 axis=1)
        ffn = routed + sh_ref[0]
        gt2 = mod_ref[0, 5:6, :]
        o_ref[0] = x_ref[0] + gt2 * _rms(ffn, gpost_ref[...])

        @pl.when(n + 1 == total)
        def _():
            drain(slot1)
            drain(slot2)

    for slot in range(N_GATHER_BUF):
        pl.when(n % N_GATHER_BUF == slot)(functools.partial(step, slot))


def _combine(dest_flat, gates_rows,b0, b, x1, shared, mod, w, ys, tc):
    _, s, d = x1.shape
    n_i = s // tc
    t_all = dest_flat.shape[0] // TOP_K
    dest3 = dest_flat.reshape(t_all // tc, 1, tc * TOP_K)
    t0 = b0 * n_i
    last = t0 + b * n_i - 1
    cur = lambda bb, i: (t0 + bb * n_i + i, 0, 0)
    nxt1 = lambda bb, i: (jnp.minimum(t0 + bb * n_i + i + 1, last), 0, 0)
    nxt2 = lambda bb, i: (jnp.minimum(t0 + bb * n_i + i + 2, last), 0, 0)
    return pl.pallas_call(
        functools.partial(_combine_kernel, n_b=b, n_i=n_i),
        grid=(b, n_i),
        in_specs=[
            pl.BlockSpec((1, 1, tc * TOP_K), cur, memory_space=pltpu.SMEM),
            pl.BlockSpec((1, 1, tc * TOP_K), nxt1, memory_space=pltpu.SMEM),
            pl.BlockSpec((1, 1, tc * TOP_K), nxt2, memory_space=pltpu.SMEM),
            pl.BlockSpec((tc * TOP_K, LANES), lambda bb, i: (t0 + bb * n_i + i, 0)),
            pl.BlockSpec((1, tc, d), lambda bb, i: (b0 + bb, i, 0)),
            pl.BlockSpec((1, tc, d), lambda bb, i: (b0 + bb, i, 0)),
            pl.BlockSpec((1, 6, d), lambda bb, i: (b0 + bb, 0, 0)),
            _const_spec((1, d)),
            pl.BlockSpec(memory_space=pl.ANY),
        ],
        out_specs=pl.BlockSpec((1, tc, d), lambda bb, i: (bb, i, 0)),
        out_shape=jax.ShapeDtypeStruct((b, s, d), F32),
        scratch_shapes=[pltpu.VMEM((N_GATHER_BUF, tc * TOP_K * ROW_SUB, LANES), F32),
                        pltpu.VMEM((tc * ROW_SUB, LANES), F32),
                        pltpu.SemaphoreType.DMA((N_GATHER_BUF,))],
        compiler_params=pltpu.CompilerParams(
            dimension_semantics=("arbitrary", "arbitrary"), vmem_limit_bytes=VMEM_LIMIT),
        name="combine",
    )(dest3, dest3, dest3, gates_rows, x1, shared, mod, w["g_post_ffn"].reshape(1, d), ys)


def _moe_kernel(be_ref, nv_ref, nu_ref, inv_ref, invn_ref, h2_ref, wg_ref, wu_ref, wd_ref, yp_ref,
                *scratch, n_tok):
    del nv_ref
    xb = scratch[0:2]
    nq = EXPERT_SUB_BLOCKS
    yb = (scratch[2:2 + nq], scratch[2 + nq:2 + 2 * nq])
    wg_s, wu_s, wd_s, gsem, ssem = scratch[2 + 2 * nq:]
    i = pl.program_id(0)
    nu = nu_ref[0]
    n_rows = xb[0].shape[0]
    tme = n_rows // ROW_SUB
    pair_shift = TOP_K.bit_length() - 1

    def gather(idx_ref, slot, unroll):
        def body(r, c):
            tok = jnp.minimum(idx_ref[0, 0, r] >> pair_shift, n_tok - 1)
            pltpu.make_async_copy(h2_ref.at[pl.ds(tok * ROW_SUB, ROW_SUB)],
                                  xb[slot].at[pl.ds(r * ROW_SUB, ROW_SUB)], gsem.at[slot]).start()
            return c
        lax.fori_loop(0, tme, body, 0, unroll=unroll)

    def drain_gather(slot):
        pltpu.make_async_copy(h2_ref.at[pl.ds(0, n_rows)], xb[slot], gsem.at[slot]).wait()

    def drain_scatter(slot):
        pltpu.make_async_copy(xb[slot], yp_ref.at[pl.ds(0, n_rows)], ssem.at[slot]).wait()

    def step(slot):
        @pl.when(i == 0)
        def _():
            gather(inv_ref, slot, 8)

        prev = be_ref[jnp.maximum(i - 1, 0)]
        changed = jnp.logical_or(i == 0, prev != be_ref[i])

        @pl.when(changed)
        def _():
            wg_s[...] = wg_ref[0].astype(BF16)
            wu_s[...] = wu_ref[0].astype(BF16)
            wd_s[...] = wd_ref[0].astype(BF16)

        drain_gather(slot)

        @pl.when(i >= 2)
        def _():
            drain_scatter(slot)

        sub = tme // EXPERT_SUB_BLOCKS
        for q in range(EXPERT_SUB_BLOCKS):
            r0 = q * sub * ROW_SUB
            x = jnp.concatenate(
                [xb[slot][pl.ds(r0 + s, sub, stride=ROW_SUB), :] for s in range(ROW_SUB)],
                axis=1).astype(BF16)
            g = _dot(x, wg_s[...])
            u = _dot(x, wu_s[...])
            a = (g * jax.nn.sigmoid(g) * u).astype(BF16)
            y = _dot(a, wd_s[...])
            for s in range(ROW_SUB):
                yb[slot][q][pl.ds(s, sub, stride=ROW_SUB), :] = y[:, s * LANES:(s + 1) * LANES]
            for r in range(q * sub, (q + 1) * sub):
                tok = jnp.minimum(invn_ref[0, 0, r] >> pair_shift, n_tok - 1)
                pltpu.make_async_copy(h2_ref.at[pl.ds(tok * ROW_SUB, ROW_SUB)],
                                      xb[1 - slot].at[pl.ds(r * ROW_SUB, ROW_SUB)],
                                      gsem.at[1 - slot]).start()
                pair = inv_ref[0, 0, r]
                pltpu.make_async_copy(yb[slot][q].at[pl.ds((r - q * sub) * ROW_SUB, ROW_SUB)],
                                      yp_ref.at[pl.ds(pair * ROW_SUB, ROW_SUB)],
                                      ssem.at[slot]).start()

        @pl.when(i == nu - 1)
        def _():
            drain_gather(1 - slot)
            drain_scatter(slot)

            @pl.when(nu >= 2)
            def _():
                drain_scatter(1 - slot)

    @pl.when(jnp.logical_and(i < nu, i % 2 == 0))
    def _():
        step(0)

    @pl.when(jnp.logical_and(i < nu, i % 2 == 1))
    def _():
        step(1)


def _moe(block_e, nvalid, nused, inv, h2rows, wg, wu, wd, tme, n_pairs):
    n_blocks = inv.shape[0] // tme
    n_tok = h2rows.shape[0] // ROW_SUB
    inv3 = inv.reshape(n_blocks, 1, tme)
    cur = lambda i, be, nv, nu: (jnp.minimum(i, nu[0] - 1), 0, 0)
    nxt = lambda i, be, nv, nu: (jnp.minimum(i + 1, nu[0] - 1), 0, 0)
    wsel = lambda i, be, nv, nu: (be[i], 0, 0)
    return pl.pallas_call(
        functools.partial(_moe_kernel, n_tok=n_tok),
        grid_spec=pltpu.PrefetchScalarGridSpec(
            num_scalar_prefetch=3,
            grid=(n_blocks,),
            in_specs=[pl.BlockSpec((1, 1, tme), cur, memory_space=pltpu.SMEM),
                      pl.BlockSpec((1, 1, tme), nxt, memory_space=pltpu.SMEM),
                      pl.BlockSpec(memory_space=pl.ANY),
                      pl.BlockSpec((1, D_MODEL, FF), wsel),
                      pl.BlockSpec((1, D_MODEL, FF), wsel),
                      pl.BlockSpec((1, FF, D_MODEL), wsel)],
            out_specs=pl.BlockSpec(memory_space=pl.ANY),
            scratch_shapes=[pltpu.VMEM((tme * ROW_SUB, LANES), F32) for _ in range(2)]
            + [pltpu.VMEM((tme // EXPERT_SUB_BLOCKS * ROW_SUB, LANES), F32)
               for _ in range(2 * EXPERT_SUB_BLOCKS)]
            + [pltpu.VMEM((D_MODEL, FF), BF16),
                            pltpu.VMEM((D_MODEL, FF), BF16),
                            pltpu.VMEM((FF, D_MODEL), BF16),
                            pltpu.SemaphoreType.DMA((2,)),
                            pltpu.SemaphoreType.DMA((2,))],
        ),
        out_shape=jax.ShapeDtypeStruct(((n_pairs + 2 * tme) * ROW_SUB, LANES), F32),
        compiler_params=pltpu.CompilerParams(
            dimension_semantics=("arbitrary",), vmem_limit_bytes=VMEM_LIMIT),
        name="experts",
    )(block_e, nvalid, nused, inv3, inv3, h2rows, wg, wu, wd)


def _final_kernel(yp_ref, gexp_ref, x_ref, mod_ref, gffn_ref, gpost_ref, wg_ref, wu_ref, wd_ref,
                  o_ref, rsum):
    tc = x_ref.shape[1]
    half = ROW_SUB // 2
    sh2 = mod_ref[0, 3:4, :]
    sc2 = mod_ref[0, 4:5, :]
    x1 = x_ref[0]
    hb = (_rms(x1, gffn_ref[...]) * (1.0 + sc2) + sh2).astype(BF16)
    sg = _dot(hb, wg_ref[...])
    su = _dot(hb, wu_ref[...])
    shared = _dot((sg * jax.nn.sigmoid(sg) * su).astype(BF16), wd_ref[...])

    for j in range(tc):
        base = j * (TOP_K * ROW_SUB)
        gk = jnp.broadcast_to(gexp_ref[pl.ds(j * TOP_K, 1), :], (half, LANES))
        acc0 = gk * yp_ref[pl.ds(base, half), :]
        acc1 = gk * yp_ref[pl.ds(base + half, half), :]
        for k in range(1, TOP_K):
            gk = jnp.broadcast_to(gexp_ref[pl.ds(j * TOP_K + k, 1), :], (half, LANES))
            acc0 = acc0 + gk * yp_ref[pl.ds(base + k * ROW_SUB, half), :]
            acc1 = acc1 + gk * yp_ref[pl.ds(base + k * ROW_SUB + half, half), :]
        rsum[pl.ds(j * ROW_SUB, half), :] = acc0
        rsum[pl.ds(j * ROW_SUB + half, half), :] = acc1
    routed = jnp.concatenate([rsum[pl.ds(s, tc, stride=ROW_SUB), :] for s in range(ROW_SUB)], axis=1)
    ffn = routed + shared
    gt2 = mod_ref[0, 5:6, :]
    o_ref[0] = x1 + gt2 * _rms(ffn, gpost_ref[...])


def _final(ypairs, gates_rows, b0, b, x1, mod, w, tc):
    _, s, d = x1.shape
    n_i = s // tc
    t0 = b0 * n_i
    return pl.pallas_call(
        _final_kernel,
        grid=(b, n_i),
        in_specs=[
            pl.BlockSpec((tc * TOP_K * ROW_SUB, LANES), lambda bb, i: (t0 + bb * n_i + i, 0)),
            pl.BlockSpec((tc * TOP_K, LANES), lambda bb, i: (t0 + bb * n_i + i, 0)),
            pl.BlockSpec((1, tc, d), lambda bb, i: (b0 + bb, i, 0)),
            pl.BlockSpec((1, 6, d), lambda bb, i: (b0 + bb, 0, 0)),
            _const_spec((1, d)),
            _const_spec((1, d)),
            _const_spec((d, FF)),
            _const_spec((d, FF)),
            _const_spec((FF, d)),
        ],
        out_specs=pl.BlockSpec((1, tc, d), lambda bb, i: (bb, i, 0)),
        out_shape=jax.ShapeDtypeStruct((b, s, d), F32),
        scratch_shapes=[pltpu.VMEM((tc * ROW_SUB, LANES), F32)],
        compiler_params=pltpu.CompilerParams(
            dimension_semantics=("arbitrary", "arbitrary"), vmem_limit_bytes=VMEM_LIMIT),
        name="final",
    )(ypairs, gates_rows, x1, mod, w["g_pre_ffn"].reshape(1, d), w["g_post_ffn"].reshape(1, d),
      w["w_sh_gate"], w["w_sh_up"], w["w_sh_down"])


def _tile_sizes(s):
    return dict(tm_mix=min(256, s), tm_route=min(512, s), td=min(256, s), tme=min(512, s),
                tc=min(128, s))


def _layer(x_prompt, x_sample, c_prompt, c_sample, lw):
    bp, s, d = x_prompt.shape
    bs = x_sample.shape[0]
    ts = _tile_sizes(s)

    nb = bp + bs
    nb_pad = -(-nb // 8) * 8
    c_all = jnp.concatenate([c_prompt, c_sample, jnp.zeros((nb_pad - nb, d), F32)], axis=0)
    mod = _ada(c_all, lw["w_ada"], lw["b_ada"])[:nb].reshape(nb, 6, d)

    w = dict(lw)
    w["w_in"] = lw["w_in"].astype(BF16)
    w["w_out"] = lw["w_out"].astype(BF16)
    w["w_spatial"] = lw["w_spatial"].astype(BF16)
    w["b_spatial_t"] = lw["b_spatial"].T
    w["w_sh_gate"] = lw["w_sh_gate"].astype(BF16)
    w["w_sh_up"] = lw["w_sh_up"].astype(BF16)
    w["w_sh_down"] = lw["w_sh_down"].astype(BF16)

    x1 = _mix(x_prompt, mod, w, ts["tm_mix"], 0, nb)
    x1 = _mix(x_sample, mod, w, ts["tm_mix"], bp, nb, x1_prev=x1)

    perm = (jnp.arange(N_EXP) % N_GRP) * GRP_SZ + jnp.arange(N_EXP) // N_GRP
    wr = lw["w_router"][:, perm]
    wr_hi = wr.astype(BF16)
    wr_lo = (wr - wr_hi.astype(F32)).astype(BF16)
    br = lw["router_bias"][perm].reshape(N_EXP, 1)
    h2rows, idx, pos, gates, cnt_all = _router(x1, mod, w, wr_hi, wr_lo, br, ts["tm_route"])

    tme = ts["tme"]
    t_all = nb * s
    counts = cnt_all[:, 0].astype(I32).reshape(GRP_SZ, N_GRP).T.reshape(N_EXP)
    padded = (counts + tme - 1) // tme * tme
    pad_end = jnp.cumsum(padded)
    off = pad_end - padded
    gates_rows = jnp.broadcast_to(gates.T.reshape(t_all * TOP_K, 1), (t_all * TOP_K, LANES))
    e_ids = jnp.arange(N_EXP, dtype=I32)
    off_of_idx = jnp.sum(jnp.where(idx[:, :, None] == e_ids, off, 0), axis=-1)
    dest_flat = (off_of_idx + pos).T.reshape(t_all * TOP_K)
    n_blocks = -(-t_all * TOP_K // tme) + N_EXP
    nused = (pad_end[-1] // tme).astype(I32)
    blk_ids = jnp.minimum(jnp.arange(n_blocks, dtype=I32), nused - 1)
    blk_start = blk_ids * tme
    block_e = jnp.minimum(jnp.sum((pad_end[None, :] <= blk_start[:, None]).astype(I32), axis=1),
                          N_EXP - 1)
    be_hot = block_e[:, None] == e_ids
    cnt_of_blk = jnp.sum(jnp.where(be_hot, counts, 0), axis=1)
    off_of_blk = jnp.sum(jnp.where(be_hot, off, 0), axis=1)
    nvalid = jnp.clip(cnt_of_blk - (blk_start - off_of_blk), 0, tme).astype(I32)

    inv = _invert_slots(dest_flat, n_blocks * tme, 2 * tme)
    ypairs = _moe(block_e, nvalid, nused.reshape(1), inv, h2rows, lw["w_exp_gate"],
                  lw["w_exp_up"], lw["w_exp_down"], tme, t_all * TOP_K)

    y_p = _final(ypairs, gates_rows, 0, bp, x1, mod, w, ts["tc"])
    y_s = _final(ypairs, gates_rows, bp, bs, x1, mod, w, ts["tc"])
    return y_p, y_s


def kernel(x_prompt, x_sample, c_prompt, c_sample, w_ada, b_ada, g_pre_mix, w_in, conv_w, ln_v_g, ln_v_b, w_spatial, b_spatial, g_out_a, g_out_b, w_out, g_post_mix, g_pre_ffn, w_router, router_bias, w_exp_gate, w_exp_up, w_exp_down, w_sh_gate, w_sh_up, w_sh_down, g_post_ffn):
    names = ("w_ada", "b_ada", "g_pre_mix", "w_in", "conv_w", "ln_v_g", "ln_v_b", "w_spatial",
             "b_spatial", "g_out_a", "g_out_b", "w_out", "g_post_mix", "g_pre_ffn", "w_router",
             "router_bias", "w_exp_gate", "w_exp_up", "w_exp_down", "w_sh_gate", "w_sh_up",
             "w_sh_down", "g_post_ffn")
    stacked = (w_ada, b_ada, g_pre_mix, w_in, conv_w, ln_v_g, ln_v_b, w_spatial, b_spatial, g_out_a,
               g_out_b, w_out, g_post_mix, g_pre_ffn, w_router, router_bias, w_exp_gate, w_exp_up,
               w_exp_down, w_sh_gate, w_sh_up, w_sh_down, g_post_ffn)
    y_p, y_s = x_prompt, x_sample
    for l in range(w_ada.shape[0]):
        lw = {n: a[l] for n, a in zip(names, stacked)}
        y_p, y_s = _layer(y_p, y_s, c_prompt, c_sample, lw)
    return (y_p, y_s)
```

```python
import functools

import jax
import jax.numpy as jnp
from jax import lax
from jax.experimental import pallas as pl
from jax.experimental.pallas import tpu as pltpu
from jax.experimental.pallas import tpu_sc as plsc

F32 = jnp.float32
BF16 = jnp.bfloat16
I32 = jnp.int32

D_MODEL = 2048
CONV_W = 1024
SGU_W = 1024
HEADS = 8
HEAD_DIM = 128
CHUNK = 128
N_EXP = 64
TOP_K = 8
N_GRP = 8
GRP_SZ = 8
TOP_GRP = 4
FF = 512
EPS = 1e-6
ROUTED_SCALE = 2.5
LANES = 128
ROW_SUB = D_MODEL // LANES
HALO = 8
EXPERT_SUB_BLOCKS = 4
N_GATHER_BUF = 3

VMEM_LIMIT = 56 * 1024 * 1024


def _rms(x, g):
    return x * lax.rsqrt(jnp.mean(x * x, axis=-1, keepdims=True) + EPS) * g


def _dot(a, b):
    return jnp.dot(a, b, preferred_element_type=F32)


def _const_spec(shape):
    nd = len(shape)
    return pl.BlockSpec(shape, lambda *_: (0,) * nd, pipeline_mode=pl.Buffered(1))


def _ada_kernel(c_ref, w_ref, b_ref, o_ref):
    c = c_ref[...]
    s = c * jax.nn.sigmoid(c)
    o_ref[...] = jnp.dot(s, w_ref[...], precision=lax.Precision.HIGHEST,
                         preferred_element_type=F32) + b_ref[...]


def _ada(c_pad, w_ada, b_ada):
    n = w_ada.shape[1]
    bn = 1536
    return pl.pallas_call(
        _ada_kernel,
        grid=(n // bn,),
        in_specs=[pl.BlockSpec(c_pad.shape, lambda j: (0, 0)),
                  pl.BlockSpec((D_MODEL, bn), lambda j: (0, j)),
                  pl.BlockSpec((1, bn), lambda j: (0, j))],
        out_specs=pl.BlockSpec((c_pad.shape[0], bn), lambda j: (0, j)),
        out_shape=jax.ShapeDtypeStruct((c_pad.shape[0], n), F32),
        compiler_params=pltpu.CompilerParams(vmem_limit_bytes=VMEM_LIMIT),
        name="ada_mod",
    )(c_pad, w_ada, b_ada.reshape(1, n))


def _mix_kernel(x_ref, xp_ref, xn_ref, mod_ref, gpre_ref, win_ref, cw_ref, lng_ref, lnb_ref,
                wsp_ref, bsp_ref, ga_ref, gb_ref, wout_ref, gpost_ref, *rest, n_i):
    o_ref, zbuf = rest[-2:]
    i = pl.program_id(1)
    tm = x_ref.shape[1]
    sh1 = mod_ref[0, 0:1, :]
    sc1 = mod_ref[0, 1:2, :]
    gt1 = mod_ref[0, 2:3, :]
    gpre = gpre_ref[...]

    def prenorm(xv):
        return _rms(xv, gpre) * (1.0 + sc1) + sh1

    x = x_ref[0]
    h = jnp.concatenate([prenorm(x), prenorm(xp_ref[0]), prenorm(xn_ref[0])], axis=0).astype(BF16)
    ht = h[:tm]

    cz = _dot(h, win_ref[:, CONV_W:3 * CONV_W])
    z = cz[:, :CONV_W] * cz[:, CONV_W:]
    zbuf[HALO:HALO + tm, :] = z[:tm]
    zbuf[0:HALO, :] = jnp.where(i == 0, 0.0, z[tm:tm + HALO])
    zbuf[HALO + tm:, :] = jnp.where(i == n_i - 1, 0.0, z[tm + HALO:])
    cw = cw_ref[...]
    conv = (zbuf[HALO - 1:HALO - 1 + tm, :] * cw[0:1] + zbuf[HALO:HALO + tm, :] * cw[1:2]
            + zbuf[HALO + 1:HALO + 1 + tm, :] * cw[2:3])
    bg = _dot(ht, win_ref[:, :CONV_W])
    ya = _rms(bg * conv, ga_ref[...])

    uv = _dot(ht, win_ref[:, 3 * CONV_W:])
    u = jax.nn.gelu(uv[:, :SGU_W], approximate=True)
    v = jax.nn.gelu(uv[:, SGU_W:], approximate=True)
    lng = lng_ref[...]
    lnb = lnb_ref[...]
    nchunk = tm // CHUNK
    parts = []
    for hd in range(HEADS):
        lo, hi = hd * HEAD_DIM, (hd + 1) * HEAD_DIM
        vh = v[:, lo:hi]
        mu = jnp.mean(vh, axis=-1, keepdims=True)
        vc = vh - mu
        var = jnp.mean(vc * vc, axis=-1, keepdims=True)
        vn = vc * lax.rsqrt(var + EPS) * lng[:, lo:hi] + lnb[:, lo:hi]
        vcat = jnp.concatenate([vn[c * CHUNK:(c + 1) * CHUNK] for c in range(nchunk)],
                               axis=1).astype(BF16)
        sp = _dot(wsp_ref[hd], vcat) + bsp_ref[:, hd:hd + 1]
        sp_rows = jnp.concatenate([sp[:, c * HEAD_DIM:(c + 1) * HEAD_DIM] for c in range(nchunk)],
                                  axis=0)
        parts.append(u[:, lo:hi] * sp_rows)
    yb = _rms(jnp.concatenate(parts, axis=1), gb_ref[...])

    cat = jnp.concatenate([ya, yb], axis=1).astype(BF16)
    mix = _dot(cat, wout_ref[...])
    o_ref[0] = x + gt1 * _rms(mix, gpost_ref[...])


def _mix(x, mod, w, tm, b0, b_total, x1_prev=None):
    b, s, d = x.shape
    n_i = s // tm
    hb = tm // HALO
    n_hb = s // HALO
    row = lambda a: a.reshape(1, -1)
    extra_specs, extra_args, aliases = [], [], {}
    if x1_prev is not None:
        extra_specs, extra_args, aliases = [pl.BlockSpec(memory_space=pl.ANY)], [x1_prev], {15: 0}
    return pl.pallas_call(
        functools.partial(_mix_kernel, n_i=n_i),
        grid=(b, n_i),
        in_specs=[
            pl.BlockSpec((1, tm, d), lambda bb, i: (bb, i, 0)),
            pl.BlockSpec((1, HALO, d), lambda bb, i: (bb, jnp.maximum(i * hb - 1, 0), 0)),
            pl.BlockSpec((1, HALO, d), lambda bb, i: (bb, jnp.minimum((i + 1) * hb, n_hb - 1), 0)),
            pl.BlockSpec((1, 6, d), lambda bb, i: (b0 + bb, 0, 0)),
            _const_spec((1, d)),
            _const_spec(w["w_in"].shape),
            _const_spec((3, CONV_W)),
            _const_spec((1, SGU_W)),
            _const_spec((1, SGU_W)),
            _const_spec((HEADS, CHUNK, CHUNK)),
            _const_spec((CHUNK, HEADS)),
            _const_spec((1, CONV_W)),
            _const_spec((1, SGU_W)),
            _const_spec((d, d)),
            _const_spec((1, d)),
        ] + extra_specs,
        out_specs=pl.BlockSpec((1, tm, d), lambda bb, i: (b0 + bb, i, 0)),
        out_shape=jax.ShapeDtypeStruct((b_total, s, d), F32),
        input_output_aliases=aliases,
        scratch_shapes=[pltpu.VMEM((tm + 2 * HALO, CONV_W), F32)],
        compiler_params=pltpu.CompilerParams(
            dimension_semantics=("arbitrary", "arbitrary"), vmem_limit_bytes=VMEM_LIMIT),
        name="token_mix",
    )(x, x, x, mod, row(w["g_pre_mix"]), w["w_in"], w["conv_w"], row(w["ln_v_g"]), row(w["ln_v_b"]),
      w["w_spatial"], w["b_spatial_t"], row(w["g_out_a"]), row(w["g_out_b"]), w["w_out"],
      row(w["g_post_mix"]), *extra_args)


def _router_kernel(x_ref, mod_ref, g_ref, whi_ref, wlo_ref, br_ref,
                   h2_ref, idx_ref, pos_ref, gate_ref, cnt_ref, carry):
    first = jnp.logical_and(pl.program_id(0) == 0, pl.program_id(1) == 0)

    @pl.when(first)
    def _():
        carry[...] = jnp.zeros_like(carry)

    tm = x_ref.shape[1]
    sh2 = mod_ref[0, 3:4, :]
    sc2 = mod_ref[0, 4:5, :]
    h2 = _rms(x_ref[0], g_ref[...]) * (1.0 + sc2) + sh2
    for s in range(ROW_SUB):
        h2_ref[pl.ds(s, tm, stride=ROW_SUB), :] = h2[:, s * LANES:(s + 1) * LANES]

    h_hi = h2.astype(BF16)
    h_lo = (h2 - h_hi.astype(F32)).astype(BF16)
    w_hi = whi_ref[...]
    logits_t = _dot(h_hi, w_hi) + (_dot(h_lo, w_hi) + _dot(h_hi, wlo_ref[...]))
    logits = logits_t.T
    scores = jax.nn.sigmoid(logits)
    sel = scores + br_ref[...]

    ri = lax.broadcasted_iota(I32, (N_EXP, LANES), 0)
    e_map = (ri % N_GRP) * GRP_SZ + ri // N_GRP
    gi = lax.broadcasted_iota(I32, (N_GRP, LANES), 0)

    n_chunk = tm // LANES
    sel_chunks, idx_chunks, score_chunks = [], [], []
    for c in range(n_chunk):
        sel_c = sel[:, c * LANES:(c + 1) * LANES]
        sc_c = scores[:, c * LANES:(c + 1) * LANES]
        m1 = sel_c[0:N_GRP]
        m2 = jnp.full_like(m1, -jnp.inf)
        for j in range(1, GRP_SZ):
            vj = sel_c[j * N_GRP:(j + 1) * N_GRP]
            m2 = jnp.maximum(m2, jnp.minimum(m1, vj))
            m1 = jnp.maximum(m1, vj)
        gs = m1 + m2
        grank = jnp.zeros(gs.shape, I32)
        for g2 in range(N_GRP):
            rowv = gs[g2:g2 + 1, :]
            beats = jnp.logical_or(rowv > gs, jnp.logical_and(rowv == gs, g2 < gi))
            grank = grank + beats.astype(I32)
        gmask = grank < TOP_GRP
        m = jnp.concatenate(
            [jnp.where(gmask, sel_c[j * N_GRP:(j + 1) * N_GRP], -jnp.inf) for j in range(GRP_SZ)],
            axis=0)
        chosen = jnp.zeros(m.shape, jnp.bool_)
        idx_rows, score_rows = [], []
        for _ in range(TOP_K):
            mx = jnp.max(m, axis=0, keepdims=True)
            e_min = jnp.min(jnp.where(m == mx, e_map, N_EXP), axis=0, keepdims=True)
            pick = e_map == e_min
            chosen = jnp.logical_or(chosen, pick)
            m = jnp.where(pick, -jnp.inf, m)
            idx_rows.append(e_min)
            score_rows.append(jnp.sum(jnp.where(pick, sc_c, 0.0), axis=0, keepdims=True))
        sel_chunks.append(chosen.astype(F32))
        idx_chunks.append(jnp.concatenate(idx_rows, axis=0))
        score_chunks.append(jnp.concatenate(score_rows, axis=0))
    self32 = jnp.concatenate(sel_chunks, axis=1)
    idx_k = jnp.concatenate(idx_chunks, axis=1)
    score_k = jnp.concatenate(score_chunks, axis=1)
    wsum = jnp.sum(score_k, axis=0, keepdims=True)
    gate_ref[...] = score_k / (wsum + 1e-20) * ROUTED_SCALE
    idx_ref[...] = idx_k

    selb = self32.astype(BF16)
    tr = lax.broadcasted_iota(I32, (tm, tm), 0)
    tc = lax.broadcasted_iota(I32, (tm, tm), 1)
    upper = (tr <= tc).astype(BF16)
    incl = _dot(selb, upper)
    pos_full = incl - self32 + carry[...]
    carry[...] = carry[...] + incl[:, tm - 1:tm]

    pos_chunks = []
    for c in range(n_chunk):
        pos_c = pos_full[:, c * LANES:(c + 1) * LANES]
        rows = []
        for k in range(TOP_K):
            pick = e_map == idx_k[k:k + 1, c * LANES:(c + 1) * LANES]
            rows.append(jnp.sum(jnp.where(pick, pos_c, 0.0), axis=0, keepdims=True))
        pos_chunks.append(jnp.concatenate(rows, axis=0))
    pos_ref[...] = jnp.concatenate(pos_chunks, axis=1).astype(I32)
    cnt_ref[...] = jnp.broadcast_to(carry[...], cnt_ref.shape)


def _router(x1, mod, w, w_hi, w_lo, br, tm):
    b, s, d = x1.shape
    n_i = s // tm
    t = b * s
    return pl.pallas_call(
        _router_kernel,
        grid=(b, n_i),
        in_specs=[
            pl.BlockSpec((1, tm, d), lambda bb, i: (bb, i, 0)),
            pl.BlockSpec((1, 6, d), lambda bb, i: (bb, 0, 0)),
            _const_spec((1, d)),
            _const_spec((d, N_EXP)),
            _const_spec((d, N_EXP)),
            _const_spec((N_EXP, 1)),
        ],
        out_specs=[
            pl.BlockSpec((tm * ROW_SUB, LANES), lambda bb, i: (bb * n_i + i, 0)),
            pl.BlockSpec((TOP_K, tm), lambda bb, i: (0, bb * n_i + i)),
            pl.BlockSpec((TOP_K, tm), lambda bb, i: (0, bb * n_i + i)),
            pl.BlockSpec((TOP_K, tm), lambda bb, i: (0, bb * n_i + i)),
            pl.BlockSpec((N_EXP, LANES), lambda bb, i: (0, 0)),
        ],
        out_shape=[
            jax.ShapeDtypeStruct((t * ROW_SUB, LANES), F32),
            jax.ShapeDtypeStruct((TOP_K, t), I32),
            jax.ShapeDtypeStruct((TOP_K, t), I32),
            jax.ShapeDtypeStruct((TOP_K, t), F32),
            jax.ShapeDtypeStruct((N_EXP, LANES), F32),
        ],
        scratch_shapes=[pltpu.VMEM((N_EXP, 1), F32)],
        compiler_params=pltpu.CompilerParams(
            dimension_semantics=("arbitrary", "arbitrary"), vmem_limit_bytes=VMEM_LIMIT),
        name="router",
    )(x1, mod, w["g_pre_ffn"].reshape(1, d), w_hi, w_lo, br)


def _slot_tokens(dest_flat, n_slots):
    info = plsc.get_sparse_core_info()
    n_workers = info.num_cores * info.num_subcores
    lanes = info.num_lanes
    n_pairs = dest_flat.shape[0]
    per_worker = n_slots // n_workers
    chunk = min(n_pairs, 16384)
    pair_shift = TOP_K.bit_length() - 1
    assert n_slots % (n_workers * lanes) == 0 and n_pairs % chunk == 0 and chunk % lanes == 0
    mesh = plsc.VectorSubcoreMesh(core_axis_name="core", subcore_axis_name="subcore")

    def body(dest_hbm, tok_hbm, dbuf, obuf):
        wid = lax.axis_index("subcore") * info.num_cores + lax.axis_index("core")
        lo = wid * per_worker

        @pl.loop(0, per_worker, step=lanes)
        def _(r):
            obuf[pl.ds(r, lanes)] = jnp.zeros((lanes,), I32)

        @pl.loop(0, n_pairs, step=chunk)
        def _(c0):
            pltpu.sync_copy(dest_hbm.at[pl.ds(c0, chunk)], dbuf)

            @pl.loop(0, chunk, step=lanes)
            def _(j):
                loc = dbuf[pl.ds(j, lanes)] - lo
                mine = jnp.logical_and(loc >= 0, loc < per_worker)
                tok = lax.shift_right_logical(c0 + j + lax.iota(I32, lanes), pair_shift)
                plsc.store_scatter(obuf, [jnp.where(mine, loc, 0)], tok, mask=mine)

        pltpu.sync_copy(obuf, tok_hbm.at[pl.ds(lo, per_worker)])

    return pl.kernel(
        body, out_type=jax.ShapeDtypeStruct((n_slots,), I32), mesh=mesh,
        scratch_types=[pltpu.VMEM((chunk,), I32), pltpu.VMEM((per_worker,), I32)],
        compiler_params=pltpu.CompilerParams(needs_layout_passes=False),
        name="slot_tokens",
    )(dest_flat)


def _expert_kernel(be_ref, nu_ref, tok_ref, tokn_ref, h2_ref, wg_ref, wu_ref, wd_ref, ys_ref,
                   xb0, xb1, wg_s, wu_s, wd_s, gsem):
    xb = (xb0, xb1)
    i = pl.program_id(0)
    nu = nu_ref[0]
    n_rows = xb0.shape[0]
    tme = n_rows // ROW_SUB

    def request(idx_ref, slot, r):
        tok = idx_ref[0, 0, r]
        pltpu.make_async_copy(h2_ref.at[pl.ds(tok * ROW_SUB, ROW_SUB)],
                              xb[slot].at[pl.ds(r * ROW_SUB, ROW_SUB)], gsem.at[slot]).start()

    def drain(slot):
        pltpu.make_async_copy(h2_ref.at[pl.ds(0, n_rows)], xb[slot], gsem.at[slot]).wait()

    def step(slot):
        @pl.when(i == 0)
        def _():
            def body(r, c):
                request(tok_ref, slot, r)
                return c
            lax.fori_loop(0, tme, body, 0, unroll=8)

        prev = be_ref[jnp.maximum(i - 1, 0)]
        changed = jnp.logical_or(i == 0, prev != be_ref[i])

        @pl.when(changed)
        def _():
            wg_s[...] = wg_ref[0].astype(BF16)
            wu_s[...] = wu_ref[0].astype(BF16)
            wd_s[...] = wd_ref[0].astype(BF16)

        drain(slot)

        sub = tme // EXPERT_SUB_BLOCKS
        for q in range(EXPERT_SUB_BLOCKS):
            r0 = q * sub * ROW_SUB
            x = jnp.concatenate(
                [xb[slot][pl.ds(r0 + s, sub, stride=ROW_SUB), :] for s in range(ROW_SUB)],
                axis=1).astype(BF16)
            g = _dot(x, wg_s[...])
            u = _dot(x, wu_s[...])
            a = (g * jax.nn.sigmoid(g) * u).astype(BF16)
            y = _dot(a, wd_s[...])
            for s in range(ROW_SUB):
                ys_ref[pl.ds(r0 + s, sub, stride=ROW_SUB), :] = y[:, s * LANES:(s + 1) * LANES]
            for r in range(q * sub, (q + 1) * sub):
                request(tokn_ref, 1 - slot, r)

        @pl.when(i == nu - 1)
        def _():
            drain(1 - slot)

    @pl.when(jnp.logical_and(i < nu, i % 2 == 0))
    def _():
        step(0)

    @pl.when(jnp.logical_and(i < nu, i % 2 == 1))
    def _():
        step(1)


def _experts(block_e, nused, slot_tok, h2rows, wg, wu, wd, tme):
    n_blocks = slot_tok.shape[0] // tme
    tok3 = slot_tok.reshape(n_blocks, 1, tme)
    cur3 = lambda i, be, nu: (jnp.minimum(i, nu[0] - 1), 0, 0)
    nxt3 = lambda i, be, nu: (jnp.minimum(i + 1, nu[0] - 1), 0, 0)
    cur2 = lambda i, be, nu: (jnp.minimum(i, nu[0] - 1), 0)
    wsel = lambda i, be, nu: (be[i], 0, 0)
    return pl.pallas_call(
        _expert_kernel,
        grid_spec=pltpu.PrefetchScalarGridSpec(
            num_scalar_prefetch=2,
            grid=(n_blocks,),
            in_specs=[pl.BlockSpec((1, 1, tme), cur3, memory_space=pltpu.SMEM),
                      pl.BlockSpec((1, 1, tme), nxt3, memory_space=pltpu.SMEM),
                      pl.BlockSpec(memory_space=pl.ANY),
                      pl.BlockSpec((1, D_MODEL, FF), wsel),
                      pl.BlockSpec((1, D_MODEL, FF), wsel),
                      pl.BlockSpec((1, FF, D_MODEL), wsel)],
            out_specs=pl.BlockSpec((tme * ROW_SUB, LANES), cur2),
            scratch_shapes=[pltpu.VMEM((tme * ROW_SUB, LANES), F32),
                            pltpu.VMEM((tme * ROW_SUB, LANES), F32),
                            pltpu.VMEM((D_MODEL, FF), BF16),
                            pltpu.VMEM((D_MODEL, FF), BF16),
                            pltpu.VMEM((FF, D_MODEL), BF16),
                            pltpu.SemaphoreType.DMA((2,))],
        ),
        out_shape=jax.ShapeDtypeStruct((n_blocks * tme * ROW_SUB, LANES), F32),
        compiler_params=pltpu.CompilerParams(
            dimension_semantics=("arbitrary",), vmem_limit_bytes=VMEM_LIMIT),
        name="experts",
    )(block_e, nused, tok3, tok3, h2rows, wg, wu, wd)


def _combine_kernel(dest_ref, dest1_ref, dest2_ref, gexp_ref, x_ref, mod_ref, gffn_ref, gpost_ref,
                    wg_ref, wu_ref, wd_ref, ys_ref, o_ref, buf, rsum, sem, *, n_b, n_i):
    tc = x_ref.shape[1]
    n = pl.program_id(0) * n_i + pl.program_id(1)
    total = n_b * n_i
    n_rows = tc * TOP_K * ROW_SUB
    half = ROW_SUB // 2

    def gather_rows(idx_ref, slot, unroll):
        def body(j, c):
            for k in range(TOP_K):
                src = idx_ref[0, 0, j * TOP_K + k]
                pltpu.make_async_copy(ys_ref.at[pl.ds(src * ROW_SUB, ROW_SUB)],
                                      buf.at[slot, pl.ds((j * TOP_K + k) * ROW_SUB, ROW_SUB)],
                                      sem.at[slot]).start()
            return c
        lax.fori_loop(0, tc, body, 0, unroll=unroll)

    def drain(slot):
        pltpu.make_async_copy(ys_ref.at[pl.ds(0, n_rows)], buf.at[slot], sem.at[slot]).wait()

    def step(slot):
        slot1 = (slot + 1) % N_GATHER_BUF
        slot2 = (slot + 2) % N_GATHER_BUF

        @pl.when(n == 0)
        def _():
            gather_rows(dest_ref, slot, 2)
            gather_rows(dest1_ref, slot1, 2)

        drain(slot)
        gather_rows(dest2_ref, slot2, True)

        sh2 = mod_ref[0, 3:4, :]
        sc2 = mod_ref[0, 4:5, :]
        x1 = x_ref[0]
        hb = (_rms(x1, gffn_ref[...]) * (1.0 + sc2) + sh2).astype(BF16)
        sg = _dot(hb, wg_ref[...])
        su = _dot(hb, wu_ref[...])
        shared = _dot((sg * jax.nn.sigmoid(sg) * su).astype(BF16), wd_ref[...])

        for j in range(tc):
            base = j * (TOP_K * ROW_SUB)
            gk = jnp.broadcast_to(gexp_ref[pl.ds(j * TOP_K, 1), :], (half, LANES))
            acc0 = gk * buf[slot, pl.ds(base, half), :]
            acc1 = gk * buf[slot, pl.ds(base + half, half), :]
            for k in range(1, TOP_K):
                gk = jnp.broadcast_to(gexp_ref[pl.ds(j * TOP_K + k, 1), :], (half, LANES))
                acc0 = acc0 + gk * buf[slot, pl.ds(base + k * ROW_SUB, half), :]
                acc1 = acc1 + gk * buf[slot, pl.ds(base + k * ROW_SUB + half, half), :]
            rsum[pl.ds(j * ROW_SUB, half), :] = acc0
            rsum[pl.ds(j * ROW_SUB + half, half), :] = acc1

        routed = jnp.concatenate([rsum[pl.ds(s, tc, stride=ROW_SUB), :] for s in range(ROW_SUB)],
                                 axis=1)
        ffn = routed + shared
        gt2 = mod_ref[0, 5:6, :]
        o_ref[0] = x1 + gt2 * _rms(ffn, gpost_ref[...])

        @pl.when(n + 1 == total)
        def _():
            drain(slot1)
            drain(slot2)

    for slot in range(N_GATHER_BUF):
        pl.when(n % N_GATHER_BUF == slot)(functools.partial(step, slot))


def _combine(dest_flat, gates_rows, b0, b, x1, mod, w, ys, tc):
    _, s, d = x1.shape
    n_i = s // tc
    t_all = dest_flat.shape[0] // TOP_K
    dest3 = dest_flat.reshape(t_all // tc, 1, tc * TOP_K)
    t0 = b0 * n_i
    last = t0 + b * n_i - 1
    cur = lambda bb, i: (t0 + bb * n_i + i, 0, 0)
    nxt1 = lambda bb, i: (jnp.minimum(t0 + bb * n_i + i + 1, last), 0, 0)
    nxt2 = lambda bb, i: (jnp.minimum(t0 + bb * n_i + i + 2, last), 0, 0)
    return pl.pallas_call(
        functools.partial(_combine_kernel, n_b=b, n_i=n_i),
        grid=(b, n_i),
        in_specs=[
            pl.BlockSpec((1, 1, tc * TOP_K), cur, memory_space=pltpu.SMEM),
            pl.BlockSpec((1, 1, tc * TOP_K), nxt1, memory_space=pltpu.SMEM),
            pl.BlockSpec((1, 1, tc * TOP_K), nxt2, memory_space=pltpu.SMEM),
            pl.BlockSpec((tc * TOP_K, LANES), lambda bb, i: (t0 + bb * n_i + i, 0)),
            pl.BlockSpec((1, tc, d), lambda bb, i: (b0 + bb, i, 0)),
            pl.BlockSpec((1, 6, d), lambda bb, i: (b0 + bb, 0, 0)),
            _const_spec((1, d)),
            _const_spec((1, d)),
            _const_spec((d, FF)),
            _const_spec((d, FF)),
            _const_spec((FF, d)),
            pl.BlockSpec(memory_space=pl.ANY),
        ],
        out_specs=pl.BlockSpec((1, tc, d), lambda bb, i: (bb, i, 0)),
        out_shape=jax.ShapeDtypeStruct((b, s, d), F32),
        scratch_shapes=[pltpu.VMEM((N_GATHER_BUF, tc * TOP_K * ROW_SUB, LANES), F32),
                        pltpu.VMEM((tc * ROW_SUB, LANES), F32),
                        pltpu.SemaphoreType.DMA((N_GATHER_BUF,))],
        compiler_params=pltpu.CompilerParams(
            dimension_semantics=("arbitrary", "arbitrary"), vmem_limit_bytes=VMEM_LIMIT),
        name="combine",
    )(dest3, dest3, dest3, gates_rows, x1, mod, w["g_pre_ffn"].reshape(1, d),
      w["g_post_ffn"].reshape(1, d), w["w_sh_gate"], w["w_sh_up"], w["w_sh_down"], ys)


def _tile_sizes(s):
    return dict(tm_mix=min(256, s), tm_route=min(512, s), tme=min(512, s), tc=min(128, s))


def _layer(x_prompt, x_sample, c_prompt, c_sample, lw):
    bp, s, d = x_prompt.shape
    bs = x_sample.shape[0]
    ts = _tile_sizes(s)

    nb = bp + bs
    nb_pad = -(-nb // 8) * 8
    c_all = jnp.concatenate([c_prompt, c_sample, jnp.zeros((nb_pad - nb, d), F32)], axis=0)
    mod = _ada(c_all, lw["w_ada"], lw["b_ada"])[:nb].reshape(nb, 6, d)

    w = dict(lw)
    w["w_in"] = lw["w_in"].astype(BF16)
    w["w_out"] = lw["w_out"].astype(BF16)
    w["w_spatial"] = lw["w_spatial"].astype(BF16)
    w["b_spatial_t"] = lw["b_spatial"].T
    w["w_sh_gate"] = lw["w_sh_gate"].astype(BF16)
    w["w_sh_up"] = lw["w_sh_up"].astype(BF16)
    w["w_sh_down"] = lw["w_sh_down"].astype(BF16)

    x1 = _mix(x_prompt, mod, w, ts["tm_mix"], 0, nb)
    x1 = _mix(x_sample, mod, w, ts["tm_mix"], bp, nb, x1_prev=x1)

    perm = (jnp.arange(N_EXP) % N_GRP) * GRP_SZ + jnp.arange(N_EXP) // N_GRP
    wr = lw["w_router"][:, perm]
    wr_hi = wr.astype(BF16)
    wr_lo = (wr - wr_hi.astype(F32)).astype(BF16)
    br = lw["router_bias"][perm].reshape(N_EXP, 1)
    h2rows, idx, pos, gates, cnt_all = _router(x1, mod, w, wr_hi, wr_lo, br, ts["tm_route"])

    tme = ts["tme"]
    t_all = nb * s
    counts = cnt_all[:, 0].astype(I32).reshape(GRP_SZ, N_GRP).T.reshape(N_EXP)
    padded = (counts + tme - 1) // tme * tme
    pad_end = jnp.cumsum(padded)
    off = pad_end - padded
    gates_rows = jnp.broadcast_to(gates.T.reshape(t_all * TOP_K, 1), (t_all * TOP_K, LANES))
    e_ids = jnp.arange(N_EXP, dtype=I32)
    off_of_idx = jnp.sum(jnp.where(idx[:, :, None] == e_ids, off, 0), axis=-1)
    dest_flat = (off_of_idx + pos).T.reshape(t_all * TOP_K)
    n_blocks = -(-t_all * TOP_K // tme) + N_EXP
    nused = (pad_end[-1] // tme).astype(I32)
    blk_ids = jnp.minimum(jnp.arange(n_blocks, dtype=I32), nused - 1)
    block_e = jnp.minimum(
        jnp.sum((pad_end[None, :] <= (blk_ids * tme)[:, None]).astype(I32), axis=1), N_EXP - 1)

    slot_tok = _slot_tokens(dest_flat, n_blocks * tme)
    ys = _experts(block_e, nused.reshape(1), slot_tok, h2rows, lw["w_exp_gate"], lw["w_exp_up"],
                  lw["w_exp_down"], tme)

    y_p = _combine(dest_flat, gates_rows, 0, bp, x1, mod, w, ys, ts["tc"])
    y_s = _combine(dest_flat, gates_rows, bp, bs, x1, mod, w, ys, ts["tc"])
    return y_p, y_s


def kernel(x_prompt, x_sample, c_prompt, c_sample, w_ada, b_ada, g_pre_mix, w_in, conv_w, ln_v_g, ln_v_b, w_spatial, b_spatial, g_out_a, g_out_b, w_out, g_post_mix, g_pre_ffn, w_router, router_bias, w_exp_gate, w_exp_up, w_exp_down, w_sh_gate, w_sh_up, w_sh_down, g_post_ffn):
    names = ("w_ada", "b_ada", "g_pre_mix", "w_in", "conv_w", "ln_v_g", "ln_v_b", "w_spatial",
             "b_spatial", "g_out_a", "g_out_b", "w_out", "g_post_mix", "g_pre_ffn", "w_router",
             "router_bias", "w_exp_gate", "w_exp_up", "w_exp_down", "w_sh_gate", "w_sh_up",
             "w_sh_down", "g_post_ffn")
    stacked = (w_ada, b_ada, g_pre_mix, w_in, conv_w, ln_v_g, ln_v_b, w_spatial, b_spatial, g_out_a,
               g_out_b, w_out, g_post_mix, g_pre_ffn, w_router, router_bias, w_exp_gate, w_exp_up,
               w_exp_down, w_sh_gate, w_sh_up, w_sh_down, g_post_ffn)
    y_p, y_s = x_prompt, x_sample
    for l in range(w_ada.shape[0]):
        lw = {n: a[l] for n, a in zip(names, stacked)}
        y_p, y_s = _layer(y_p, y_s, c_prompt, c_sample, lw)
    return (y_p, y_s)
```

```python
import functools

import jax
import jax.numpy as jnp
from jax import lax
from jax.experimental import pallas as pl
from jax.experimental.pallas import tpu as pltpu
from jax.experimental.pallas import tpu_sc as plsc

F32 = jnp.float32
BF16 = jnp.bfloat16
I32 = jnp.int32

D_MODEL = 2048
CONV_W = 1024
SGU_W = 1024
HEADS = 8
HEAD_DIM = 128
CHUNK = 128
N_EXP = 64
TOP_K = 8
N_GRP = 8
GRP_SZ = 8
TOP_GRP = 4
FF = 512
EPS = 1e-6
ROUTED_SCALE = 2.5
LANES = 128
ROW_SUB = D_MODEL // LANES
HALO = 8
EXPERT_SUB_BLOCKS = 4
N_GATHER_BUF = 3

VMEM_LIMIT = 56 * 1024 * 1024


def _rms(x, g):
    return x * lax.rsqrt(jnp.mean(x * x, axis=-1, keepdims=True) + EPS) * g


def _dot(a, b):
    return jnp.dot(a, b, preferred_element_type=F32)


def _const_spec(shape):
    nd = len(shape)
    return pl.BlockSpec(shape, lambda *_: (0,) * nd, pipeline_mode=pl.Buffered(1))


def _ada_kernel(c_ref, w_ref, b_ref, o_ref):
    c = c_ref[...]
    s = c * jax.nn.sigmoid(c)
    o_ref[...] = jnp.dot(s, w_ref[...], precision=lax.Precision.HIGHEST,
                         preferred_element_type=F32) + b_ref[...]


def _ada(c_pad, w_ada, b_ada):
    n = w_ada.shape[1]
    bn = 1536
    return pl.pallas_call(
        _ada_kernel,
        grid=(n // bn,),
        in_specs=[pl.BlockSpec(c_pad.shape, lambda j: (0, 0)),
                  pl.BlockSpec((D_MODEL, bn), lambda j: (0, j)),
                  pl.BlockSpec((1, bn), lambda j: (0, j))],
        out_specs=pl.BlockSpec((c_pad.shape[0], bn), lambda j: (0, j)),
        out_shape=jax.ShapeDtypeStruct((c_pad.shape[0], n), F32),
        compiler_params=pltpu.CompilerParams(vmem_limit_bytes=VMEM_LIMIT),
        name="ada_mod",
    )(c_pad, w_ada, b_ada.reshape(1, n))


def _mix_kernel(x_ref, xp_ref, xn_ref, mod_ref, gpre_ref, win_ref, cw_ref, lng_ref, lnb_ref,
                wsp_ref, bsp_ref, ga_ref, gb_ref, wout_ref, gpost_ref, *rest, n_i):
    o_ref, zbuf = rest[-2:]
    i = pl.program_id(1)
    tm = x_ref.shape[1]
    sh1 = mod_ref[0, 0:1, :]
    sc1 = mod_ref[0, 1:2, :]
    gt1 = mod_ref[0, 2:3, :]
    gpre = gpre_ref[...]

    def prenorm(xv):
        return _rms(xv, gpre) * (1.0 + sc1) + sh1

    x = x_ref[0]
    h = jnp.concatenate([prenorm(x), prenorm(xp_ref[0]), prenorm(xn_ref[0])], axis=0).astype(BF16)
    ht = h[:tm]

    cz = _dot(h, win_ref[:, CONV_W:3 * CONV_W])
    z = cz[:, :CONV_W] * cz[:, CONV_W:]
    zbuf[HALO:HALO + tm, :] = z[:tm]
    zbuf[0:HALO, :] = jnp.where(i == 0, 0.0, z[tm:tm + HALO])
    zbuf[HALO + tm:, :] = jnp.where(i == n_i - 1, 0.0, z[tm + HALO:])
    cw = cw_ref[...]
    conv = (zbuf[HALO - 1:HALO - 1 + tm, :] * cw[0:1] + zbuf[HALO:HALO + tm, :] * cw[1:2]
            + zbuf[HALO + 1:HALO + 1 + tm, :] * cw[2:3])
    bg = _dot(ht, win_ref[:, :CONV_W])
    ya = _rms(bg * conv, ga_ref[...])

    uv = _dot(ht, win_ref[:, 3 * CONV_W:])
    u = jax.nn.gelu(uv[:, :SGU_W], approximate=True)
    v = jax.nn.gelu(uv[:, SGU_W:], approximate=True)
    lng = lng_ref[...]
    lnb = lnb_ref[...]
    nchunk = tm // CHUNK
    parts = []
    for hd in range(HEADS):
        lo, hi = hd * HEAD_DIM, (hd + 1) * HEAD_DIM
        vh = v[:, lo:hi]
        mu = jnp.mean(vh, axis=-1, keepdims=True)
        vc = vh - mu
        var = jnp.mean(vc * vc, axis=-1, keepdims=True)
        vn = vc * lax.rsqrt(var + EPS) * lng[:, lo:hi] + lnb[:, lo:hi]
        vcat = jnp.concatenate([vn[c * CHUNK:(c + 1) * CHUNK] for c in range(nchunk)],
                               axis=1).astype(BF16)
        sp = _dot(wsp_ref[hd], vcat) + bsp_ref[:, hd:hd + 1]
        sp_rows = jnp.concatenate([sp[:, c * HEAD_DIM:(c + 1) * HEAD_DIM] for c in range(nchunk)],
                                  axis=0)
        parts.append(u[:, lo:hi] * sp_rows)
    yb = _rms(jnp.concatenate(parts, axis=1), gb_ref[...])

    cat = jnp.concatenate([ya, yb], axis=1).astype(BF16)
    mix = _dot(cat, wout_ref[...])
    o_ref[0] = x + gt1 * _rms(mix, gpost_ref[...])


def _mix(x, mod, w, tm, b0, b_total, x1_prev=None):
    b, s, d = x.shape
    n_i = s // tm
    hb = tm // HALO
    n_hb = s // HALO
    row = lambda a: a.reshape(1, -1)
    extra_specs, extra_args, aliases = [], [], {}
    if x1_prev is not None:
        extra_specs, extra_args, aliases = [pl.BlockSpec(memory_space=pl.ANY)], [x1_prev], {15: 0}
    return pl.pallas_call(
        functools.partial(_mix_kernel, n_i=n_i),
        grid=(b, n_i),
        in_specs=[
            pl.BlockSpec((1, tm, d), lambda bb, i: (bb, i, 0)),
            pl.BlockSpec((1, HALO, d), lambda bb, i: (bb, jnp.maximum(i * hb - 1, 0), 0)),
            pl.BlockSpec((1, HALO, d), lambda bb, i: (bb, jnp.minimum((i + 1) * hb, n_hb - 1), 0)),
            pl.BlockSpec((1, 6, d), lambda bb, i: (b0 + bb, 0, 0)),
            _const_spec((1, d)),
            _const_spec(w["w_in"].shape),
            _const_spec((3, CONV_W)),
            _const_spec((1, SGU_W)),
            _const_spec((1, SGU_W)),
            _const_spec((HEADS, CHUNK, CHUNK)),
            _const_spec((CHUNK, HEADS)),
            _const_spec((1, CONV_W)),
            _const_spec((1, SGU_W)),
            _const_spec((d, d)),
            _const_spec((1, d)),
        ] + extra_specs,
        out_specs=pl.BlockSpec((1, tm, d), lambda bb, i: (b0 + bb, i, 0)),
        out_shape=jax.ShapeDtypeStruct((b_total, s, d), F32),
        input_output_aliases=aliases,
        scratch_shapes=[pltpu.VMEM((tm + 2 * HALO, CONV_W), F32)],
        compiler_params=pltpu.CompilerParams(
            dimension_semantics=("arbitrary", "arbitrary"), vmem_limit_bytes=VMEM_LIMIT),
        name="token_mix",
    )(x, x, x, mod, row(w["g_pre_mix"]), w["w_in"], w["conv_w"], row(w["ln_v_g"]), row(w["ln_v_b"]),
      w["w_spatial"], w["b_spatial_t"], row(w["g_out_a"]), row(w["g_out_b"]), w["w_out"],
      row(w["g_post_mix"]), *extra_args)


def _router_kernel(x_ref, mod_ref, g_ref, whi_ref, wlo_ref, br_ref,
                   h2_ref, idx_ref, pos_ref, gate_ref, cnt_ref, carry):
    first = jnp.logical_and(pl.program_id(0) == 0, pl.program_id(1) == 0)

    @pl.when(first)
    def _():
        carry[...] = jnp.zeros_like(carry)

    tm = x_ref.shape[1]
    sh2 = mod_ref[0, 3:4, :]
    sc2 = mod_ref[0, 4:5, :]
    h2 = _rms(x_ref[0], g_ref[...]) * (1.0 + sc2) + sh2
    for s in range(ROW_SUB):
        h2_ref[pl.ds(s, tm, stride=ROW_SUB), :] = h2[:, s * LANES:(s + 1) * LANES]

    h_hi = h2.astype(BF16)
    h_lo = (h2 - h_hi.astype(F32)).astype(BF16)
    w_hi = whi_ref[...]
    logits_t = _dot(h_hi, w_hi) + (_dot(h_lo, w_hi) + _dot(h_hi, wlo_ref[...]))
    logits = logits_t.T
    scores = jax.nn.sigmoid(logits)
    sel = scores + br_ref[...]

    ri = lax.broadcasted_iota(I32, (N_EXP, LANES), 0)
    e_map = (ri % N_GRP) * GRP_SZ + ri // N_GRP
    gi = lax.broadcasted_iota(I32, (N_GRP, LANES), 0)

    n_chunk = tm // LANES
    sel_chunks, idx_chunks, score_chunks = [], [], []
    for c in range(n_chunk):
        sel_c = sel[:, c * LANES:(c + 1) * LANES]
        sc_c = scores[:, c * LANES:(c + 1) * LANES]
        m1 = sel_c[0:N_GRP]
        m2 = jnp.full_like(m1, -jnp.inf)
        for j in range(1, GRP_SZ):
            vj = sel_c[j * N_GRP:(j + 1) * N_GRP]
            m2 = jnp.maximum(m2, jnp.minimum(m1, vj))
            m1 = jnp.maximum(m1, vj)
        gs = m1 + m2
        grank = jnp.zeros(gs.shape, I32)
        for g2 in range(N_GRP):
            rowv = gs[g2:g2 + 1, :]
            beats = jnp.logical_or(rowv > gs, jnp.logical_and(rowv == gs, g2 < gi))
            grank = grank + beats.astype(I32)
        gmask = grank < TOP_GRP
        m = jnp.concatenate(
            [jnp.where(gmask, sel_c[j * N_GRP:(j + 1) * N_GRP], -jnp.inf) for j in range(GRP_SZ)],
            axis=0)
        chosen = jnp.zeros(m.shape, jnp.bool_)
        idx_rows, score_rows = [], []
        for _ in range(TOP_K):
            mx = jnp.max(m, axis=0, keepdims=True)
            e_min = jnp.min(jnp.where(m == mx, e_map, N_EXP), axis=0, keepdims=True)
            pick = e_map == e_min
            chosen = jnp.logical_or(chosen, pick)
            m = jnp.where(pick, -jnp.inf, m)
            idx_rows.append(e_min)
            score_rows.append(jnp.sum(jnp.where(pick, sc_c, 0.0), axis=0, keepdims=True))
        sel_chunks.append(chosen.astype(F32))
        idx_chunks.append(jnp.concatenate(idx_rows, axis=0))
        score_chunks.append(jnp.concatenate(score_rows, axis=0))
    self32 = jnp.concatenate(sel_chunks, axis=1)
    idx_k = jnp.concatenate(idx_chunks, axis=1)
    score_k = jnp.concatenate(score_chunks, axis=1)
    wsum = jnp.sum(score_k, axis=0, keepdims=True)
    gate_ref[...] = score_k / (wsum + 1e-20) * ROUTED_SCALE
    idx_ref[...] = idx_k

    selb = self32.astype(BF16)
    tr = lax.broadcasted_iota(I32, (tm, tm), 0)
    tc = lax.broadcasted_iota(I32, (tm, tm), 1)
    upper = (tr <= tc).astype(BF16)
    incl = _dot(selb, upper)
    pos_full = incl - self32 + carry[...]
    carry[...] = carry[...] + incl[:, tm - 1:tm]

    pos_chunks = []
    for c in range(n_chunk):
        pos_c = pos_full[:, c * LANES:(c + 1) * LANES]
        rows = []
        for k in range(TOP_K):
            pick = e_map == idx_k[k:k + 1, c * LANES:(c + 1) * LANES]
            rows.append(jnp.sum(jnp.where(pick, pos_c, 0.0), axis=0, keepdims=True))
        pos_chunks.append(jnp.concatenate(rows, axis=0))
    pos_ref[...] = jnp.concatenate(pos_chunks, axis=1).astype(I32)
    cnt_ref[...] = jnp.broadcast_to(carry[...], cnt_ref.shape)


def _router(x1, mod, w, w_hi, w_lo, br, tm):
    b, s, d = x1.shape
    n_i = s // tm
    t = b * s
    return pl.pallas_call(
        _router_kernel,
        grid=(b, n_i),
        in_specs=[
            pl.BlockSpec((1, tm, d), lambda bb, i: (bb, i, 0)),
            pl.BlockSpec((1, 6, d), lambda bb, i: (bb, 0, 0)),
            _const_spec((1, d)),
            _const_spec((d, N_EXP)),
            _const_spec((d, N_EXP)),
            _const_spec((N_EXP, 1)),
        ],
        out_specs=[
            pl.BlockSpec((tm * ROW_SUB, LANES), lambda bb, i: (bb * n_i + i, 0)),
            pl.BlockSpec((TOP_K, tm), lambda bb, i: (0, bb * n_i + i)),
            pl.BlockSpec((TOP_K, tm), lambda bb, i: (0, bb * n_i + i)),
            pl.BlockSpec((TOP_K, tm), lambda bb, i: (0, bb * n_i + i)),
            pl.BlockSpec((N_EXP, LANES), lambda bb, i: (0, 0)),
        ],
        out_shape=[
            jax.ShapeDtypeStruct((t * ROW_SUB, LANES), F32),
            jax.ShapeDtypeStruct((TOP_K, t), I32),
            jax.ShapeDtypeStruct((TOP_K, t), I32),
            jax.ShapeDtypeStruct((TOP_K, t), F32),
            jax.ShapeDtypeStruct((N_EXP, LANES), F32),
        ],
        scratch_shapes=[pltpu.VMEM((N_EXP, 1), F32)],
        compiler_params=pltpu.CompilerParams(
            dimension_semantics=("arbitrary", "arbitrary"), vmem_limit_bytes=VMEM_LIMIT),
        name="router",
    )(x1, mod, w["g_pre_ffn"].reshape(1, d), w_hi, w_lo, br)


def _slot_tokens(dest_flat, n_slots):
    info = plsc.get_sparse_core_info()
    n_workers = info.num_cores * info.num_subcores
    lanes = info.num_lanes
    n_pairs = dest_flat.shape[0]
    per_worker = n_slots // n_workers
    chunk = min(n_pairs, 16384)
    pair_shift = TOP_K.bit_length() - 1
    assert n_slots % (n_workers * lanes) == 0 and n_pairs % chunk == 0 and chunk % lanes == 0
    mesh = plsc.VectorSubcoreMesh(core_axis_name="core", subcore_axis_name="subcore")

    def body(dest_hbm, tok_hbm, dbuf, obuf):
        wid = lax.axis_index("subcore") * info.num_cores + lax.axis_index("core")
        lo = wid * per_worker

        @pl.loop(0, per_worker, step=lanes)
        def _(r):
            obuf[pl.ds(r, lanes)] = jnp.zeros((lanes,), I32)

        @pl.loop(0, n_pairs, step=chunk)
        def _(c0):
            pltpu.sync_copy(dest_hbm.at[pl.ds(c0, chunk)], dbuf)

            @pl.loop(0, chunk, step=lanes)
            def _(j):
                loc = dbuf[pl.ds(j, lanes)] - lo
                mine = jnp.logical_and(loc >= 0, loc < per_worker)
                tok = lax.shift_right_logical(c0 + j + lax.iota(I32, lanes), pair_shift)
                plsc.store_scatter(obuf, [jnp.where(mine, loc, 0)], tok, mask=mine)

        pltpu.sync_copy(obuf, tok_hbm.at[pl.ds(lo, per_worker)])

    return pl.kernel(
        body, out_type=jax.ShapeDtypeStruct((n_slots,), I32), mesh=mesh,
        scratch_types=[pltpu.VMEM((chunk,), I32), pltpu.VMEM((per_worker,), I32)],
        compiler_params=pltpu.CompilerParams(needs_layout_passes=False),
        name="slot_tokens",
    )(dest_flat)


def _expert_kernel(be_ref, nu_ref, tok_ref, tokn_ref, h2_ref, wg_ref, wu_ref, wd_ref, ys_ref,
                   xb0, xb1, wg_s, wu_s, wd_s, gsem):
    xb = (xb0, xb1)
    i = pl.program_id(0)
    nu = nu_ref[0]
    n_rows = xb0.shape[0]
    tme = n_rows // ROW_SUB

    def request(idx_ref, slot, r):
        tok = idx_ref[0, 0, r]
        pltpu.make_async_copy(h2_ref.at[pl.ds(tok * ROW_SUB, ROW_SUB)],
                              xb[slot].at[pl.ds(r * ROW_SUB, ROW_SUB)], gsem.at[slot]).start()

    def drain(slot):
        pltpu.make_async_copy(h2_ref.at[pl.ds(0, n_rows)], xb[slot], gsem.at[slot]).wait()

    def step(slot):
        @pl.when(i == 0)
        def _():
            def body(r, c):
                request(tok_ref, slot, r)
                return c
            lax.fori_loop(0, tme, body, 0, unroll=8)

        prev = be_ref[jnp.maximum(i - 1, 0)]
        changed = jnp.logical_or(i == 0, prev != be_ref[i])

        @pl.when(changed)
        def _():
            wg_s[...] = wg_ref[0].astype(BF16)
            wu_s[...] = wu_ref[0].astype(BF16)
            wd_s[...] = wd_ref[0].astype(BF16)

        drain(slot)

        sub = tme // EXPERT_SUB_BLOCKS

        def sub_block(q, c):
            r0 = pl.multiple_of(q * (sub * ROW_SUB), sub * ROW_SUB)
            x = jnp.concatenate(
                [xb[slot][pl.ds(r0 + s, sub, stride=ROW_SUB), :] for s in range(ROW_SUB)],
                axis=1).astype(BF16)
            g = _dot(x, wg_s[...])
            u = _dot(x, wu_s[...])
            a = (g * jax.nn.sigmoid(g) * u).astype(BF16)
            y = _dot(a, wd_s[...])
            for s in range(ROW_SUB):
                ys_ref[pl.ds(r0 + s, sub, stride=ROW_SUB), :] = y[:, s * LANES:(s + 1) * LANES]
            for r in range(sub):
                request(tokn_ref, 1 - slot, q * sub + r)
            return c
        lax.fori_loop(0, EXPERT_SUB_BLOCKS, sub_block, 0)

        @pl.when(i == nu - 1)
        def _():
            drain(1 - slot)

    @pl.when(jnp.logical_and(i < nu, i % 2 == 0))
    def _():
        step(0)

    @pl.when(jnp.logical_and(i < nu, i % 2 == 1))
    def _():
        step(1)


def _experts(block_e, nused, slot_tok, h2rows, wg, wu, wd, tme):
    n_blocks = slot_tok.shape[0] // tme
    tok3 = slot_tok.reshape(n_blocks, 1, tme)
    cur3 = lambda i, be, nu: (jnp.minimum(i, nu[0] - 1), 0, 0)
    nxt3 = lambda i, be, nu: (jnp.minimum(i + 1, nu[0] - 1), 0, 0)
    cur2 = lambda i, be, nu: (jnp.minimum(i, nu[0] - 1), 0)
    wsel = lambda i, be, nu: (be[i], 0, 0)
    return pl.pallas_call(
        _expert_kernel,
        grid_spec=pltpu.PrefetchScalarGridSpec(
            num_scalar_prefetch=2,
            grid=(n_blocks,),
            in_specs=[pl.BlockSpec((1, 1, tme), cur3, memory_space=pltpu.SMEM),
                      pl.BlockSpec((1, 1, tme), nxt3, memory_space=pltpu.SMEM),
                      pl.BlockSpec(memory_space=pl.ANY),
                      pl.BlockSpec((1, D_MODEL, FF), wsel),
                      pl.BlockSpec((1, D_MODEL, FF), wsel),
                      pl.BlockSpec((1, FF, D_MODEL), wsel)],
            out_specs=pl.BlockSpec((tme * ROW_SUB, LANES), cur2),
            scratch_shapes=[pltpu.VMEM((tme * ROW_SUB, LANES), F32),
                            pltpu.VMEM((tme * ROW_SUB, LANES), F32),
                            pltpu.VMEM((D_MODEL, FF), BF16),
                            pltpu.VMEM((D_MODEL, FF), BF16),
                            pltpu.VMEM((FF, D_MODEL), BF16),
                            pltpu.SemaphoreType.DMA((2,))],
        ),
        out_shape=jax.ShapeDtypeStruct((n_blocks * tme * ROW_SUB, LANES), F32),
        compiler_params=pltpu.CompilerParams(
            dimension_semantics=("arbitrary",), vmem_limit_bytes=VMEM_LIMIT),
        name="experts",
    )(block_e, nused, tok3, tok3, h2rows, wg, wu, wd)


def _combine_kernel(dest_ref, dest1_ref, dest2_ref, gexp_ref, x_ref, mod_ref, gffn_ref, gpost_ref,
                    wg_ref, wu_ref, wd_ref, ys_ref, o_ref, buf, rsum, sem, *, n_b, n_i):
    tc = x_ref.shape[1]
    n = pl.program_id(0) * n_i + pl.program_id(1)
    total = n_b * n_i
    n_rows = tc * TOP_K * ROW_SUB
    half = ROW_SUB // 2

    def gather_rows(idx_ref, slot, unroll):
        def body(j, c):
            for k in range(TOP_K):
                src = idx_ref[0, 0, j * TOP_K + k]
                pltpu.make_async_copy(ys_ref.at[pl.ds(src * ROW_SUB, ROW_SUB)],
                                      buf.at[slot, pl.ds((j * TOP_K + k) * ROW_SUB, ROW_SUB)],
                                      sem.at[slot]).start()
            return c
        lax.fori_loop(0, tc, body, 0, unroll=unroll)

    def drain(slot):
        pltpu.make_async_copy(ys_ref.at[pl.ds(0, n_rows)], buf.at[slot], sem.at[slot]).wait()

    def step(slot):
        slot1 = (slot + 1) % N_GATHER_BUF
        slot2 = (slot + 2) % N_GATHER_BUF

        @pl.when(n == 0)
        def _():
            gather_rows(dest_ref, slot, 2)
            gather_rows(dest1_ref, slot1, 2)

        drain(slot)
        gather_rows(dest2_ref, slot2, True)

        sh2 = mod_ref[0, 3:4, :]
        sc2 = mod_ref[0, 4:5, :]
        x1 = x_ref[0]
        hb = (_rms(x1, gffn_ref[...]) * (1.0 + sc2) + sh2).astype(BF16)
        sg = _dot(hb, wg_ref[...])
        su = _dot(hb, wu_ref[...])
        shared = _dot((sg * jax.nn.sigmoid(sg) * su).astype(BF16), wd_ref[...])

        for j in range(tc):
            base = j * (TOP_K * ROW_SUB)
            gk = jnp.broadcast_to(gexp_ref[pl.ds(j * TOP_K, 1), :], (half, LANES))
            acc0 = gk * buf[slot, pl.ds(base, half), :]
            acc1 = gk * buf[slot, pl.ds(base + half, half), :]
            for k in range(1, TOP_K):
                gk = jnp.broadcast_to(gexp_ref[pl.ds(j * TOP_K + k, 1), :], (half, LANES))
                acc0 = acc0 + gk * buf[slot, pl.ds(base + k * ROW_SUB, half), :]
                acc1 = acc1 + gk * buf[slot, pl.ds(base + k * ROW_SUB + half, half), :]
            rsum[pl.ds(j * ROW_SUB, half), :] = acc0
            rsum[pl.ds(j * ROW_SUB + half, half), :] = acc1

        routed = jnp.concatenate([rsum[pl.ds(s, tc, stride=ROW_SUB), :] for s in range(ROW_SUB)],
                                 axis=1)
        ffn = routed + shared
        gt2 = mod_ref[0, 5:6, :]
        o_ref[0] = x1 + gt2 * _rms(ffn, gpost_ref[...])

        @pl.when(n + 1 == total)
        def _():
            drain(slot1)
            drain(slot2)

    for slot in range(N_GATHER_BUF):
        pl.when(n % N_GATHER_BUF == slot)(functools.partial(step, slot))


def _combine(dest_flat, gates_rows, b0, b, x1, mod, w, ys, tc):
    _, s, d = x1.shape
    n_i = s // tc
    t_all = dest_flat.shape[0] // TOP_K
    dest3 = dest_flat.reshape(t_all // tc, 1, tc * TOP_K)
    t0 = b0 * n_i
    last = t0 + b * n_i - 1
    cur = lambda bb, i: (t0 + bb * n_i + i, 0, 0)
    nxt1 = lambda bb, i: (jnp.minimum(t0 + bb * n_i + i + 1, last), 0, 0)
    nxt2 = lambda bb, i: (jnp.minimum(t0 + bb * n_i + i + 2, last), 0, 0)
    return pl.pallas_call(
        functools.partial(_combine_kernel, n_b=b, n_i=n_i),
        grid=(b, n_i),
        in_specs=[
            pl.BlockSpec((1, 1, tc * TOP_K), cur, memory_space=pltpu.SMEM),
            pl.BlockSpec((1, 1, tc * TOP_K), nxt1, memory_space=pltpu.SMEM),
            pl.BlockSpec((1, 1, tc * TOP_K), nxt2, memory_space=pltpu.SMEM),
            pl.BlockSpec((tc * TOP_K, LANES), lambda bb, i: (t0 + bb * n_i + i, 0)),
            pl.BlockSpec((1, tc, d), lambda bb, i: (b0 + bb, i, 0)),
            pl.BlockSpec((1, 6, d), lambda bb, i: (b0 + bb, 0, 0)),
            _const_spec((1, d)),
            _const_spec((1, d)),
            _const_spec((d, FF)),
            _const_spec((d, FF)),
            _const_spec((FF, d)),
            pl.BlockSpec(memory_space=pl.ANY),
        ],
        out_specs=pl.BlockSpec((1, tc, d), lambda bb, i: (bb, i, 0)),
        out_shape=jax.ShapeDtypeStruct((b, s, d), F32),
        scratch_shapes=[pltpu.VMEM((N_GATHER_BUF, tc * TOP_K * ROW_SUB, LANES), F32),
                        pltpu.VMEM((tc * ROW_SUB, LANES), F32),
                        pltpu.SemaphoreType.DMA((N_GATHER_BUF,))],
        compiler_params=pltpu.CompilerParams(
            dimension_semantics=("arbitrary", "arbitrary"), vmem_limit_bytes=VMEM_LIMIT),
        name="combine",
    )(dest3, dest3, dest3, gates_rows, x1, mod, w["g_pre_ffn"].reshape(1, d),
      w["g_post_ffn"].reshape(1, d), w["w_sh_gate"], w["w_sh_up"], w["w_sh_down"], ys)


def _tile_sizes(s):
    return dict(tm_mix=min(256, s), tm_route=min(512, s), tme=min(512, s), tc=min(128, s))


def _layer(x_prompt, x_sample, c_prompt, c_sample, lw):
    bp, s, d = x_prompt.shape
    bs = x_sample.shape[0]
    ts = _tile_sizes(s)

    nb = bp + bs
    nb_pad = -(-nb // 8) * 8
    c_all = jnp.concatenate([c_prompt, c_sample, jnp.zeros((nb_pad - nb, d), F32)], axis=0)
    mod = _ada(c_all, lw["w_ada"], lw["b_ada"])[:nb].reshape(nb, 6, d)

    w = dict(lw)
    w["w_in"] = lw["w_in"].astype(BF16)
    w["w_out"] = lw["w_out"].astype(BF16)
    w["w_spatial"] = lw["w_spatial"].astype(BF16)
    w["b_spatial_t"] = lw["b_spatial"].T
    w["w_sh_gate"] = lw["w_sh_gate"].astype(BF16)
    w["w_sh_up"] = lw["w_sh_up"].astype(BF16)
    w["w_sh_down"] = lw["w_sh_down"].astype(BF16)

    x1 = _mix(x_prompt, mod, w, ts["tm_mix"], 0, nb)
    x1 = _mix(x_sample, mod, w, ts["tm_mix"], bp, nb, x1_prev=x1)

    perm = (jnp.arange(N_EXP) % N_GRP) * GRP_SZ + jnp.arange(N_EXP) // N_GRP
    wr = lw["w_router"][:, perm]
    wr_hi = wr.astype(BF16)
    wr_lo = (wr - wr_hi.astype(F32)).astype(BF16)
    br = lw["router_bias"][perm].reshape(N_EXP, 1)
    h2rows, idx, pos, gates, cnt_all = _router(x1, mod, w, wr_hi, wr_lo, br, ts["tm_route"])

    tme = ts["tme"]
    t_all = nb * s
    counts = cnt_all[:, 0].astype(I32).reshape(GRP_SZ, N_GRP).T.reshape(N_EXP)
    padded = (counts + tme - 1) // tme * tme
    pad_end = jnp.cumsum(padded)
    off = pad_end - padded
    gates_rows = jnp.broadcast_to(gates.T.reshape(t_all * TOP_K, 1), (t_all * TOP_K, LANES))
    e_ids = jnp.arange(N_EXP, dtype=I32)
    off_of_idx = jnp.sum(jnp.where(idx[:, :, None] == e_ids, off, 0), axis=-1)
    dest_flat = (off_of_idx + pos).T.reshape(t_all * TOP_K)
    n_blocks = -(-t_all * TOP_K // tme) + N_EXP
    nused = (pad_end[-1] // tme).astype(I32)
    blk_ids = jnp.minimum(jnp.arange(n_blocks, dtype=I32), nused - 1)
    block_e = jnp.minimum(
        jnp.sum((pad_end[None, :] <= (blk_ids * tme)[:, None]).astype(I32), axis=1), N_EXP - 1)

    slot_tok = _slot_tokens(dest_flat, n_blocks * tme)
    ys = _experts(block_e, nused.reshape(1), slot_tok, h2rows, lw["w_exp_gate"], lw["w_exp_up"],
                  lw["w_exp_down"], tme)

    y_p = _combine(dest_flat, gates_rows, 0, bp, x1, mod, w, ys, ts["tc"])
    y_s = _combine(dest_flat, gates_rows, bp, bs, x1, mod, w, ys, ts["tc"])
    return y_p, y_s


def kernel(x_prompt, x_sample, c_prompt, c_sample, w_ada, b_ada, g_pre_mix, w_in, conv_w, ln_v_g, ln_v_b, w_spatial, b_spatial, g_out_a, g_out_b, w_out, g_post_mix, g_pre_ffn, w_router, router_bias, w_exp_gate, w_exp_up, w_exp_down, w_sh_gate, w_sh_up, w_sh_down, g_post_ffn):
    names = ("w_ada", "b_ada", "g_pre_mix", "w_in", "conv_w", "ln_v_g", "ln_v_b", "w_spatial",
             "b_spatial", "g_out_a", "g_out_b", "w_out", "g_post_mix", "g_pre_ffn", "w_router",
             "router_bias", "w_exp_gate", "w_exp_up", "w_exp_down", "w_sh_gate", "w_sh_up",
             "w_sh_down", "g_post_ffn")
    stacked = (w_ada, b_ada, g_pre_mix, w_in, conv_w, ln_v_g, ln_v_b, w_spatial, b_spatial, g_out_a,
               g_out_b, w_out, g_post_mix, g_pre_ffn, w_router, router_bias, w_exp_gate, w_exp_up,
               w_exp_down, w_sh_gate, w_sh_up, w_sh_down, g_post_ffn)
    y_p, y_s = x_prompt, x_sample
    for l in range(w_ada.shape[0]):
        lw = {n: a[l] for n, a in zip(names, stacked)}
        y_p, y_s = _layer(y_p, y_s, c_prompt, c_sample, lw)
    return (y_p, y_s)
```

```python
import functools

import jax
import jax.numpy as jnp
from jax import lax
from jax.experimental import pallas as pl
from jax.experimental.pallas import tpu as pltpu

F32 = jnp.float32
BF16 = jnp.bfloat16
I32 = jnp.int32

D_MODEL = 2048
CONV_W = 1024
SGU_W = 1024
HEADS = 8
HEAD_DIM = 128
CHUNK = 128
N_EXP = 64
TOP_K = 8
N_GRP = 8
GRP_SZ = 8
TOP_GRP = 4
FF = 512
EPS = 1e-6
ROUTED_SCALE = 2.5
LANES = 128
ROW_SUB = D_MODEL // LANES
HALO = 8
EXPERT_SUB_BLOCKS = 2

VMEM_LIMIT = 56 * 1024 * 1024


def _rms(x, g):
    return x * lax.rsqrt(jnp.mean(x * x, axis=-1, keepdims=True) + EPS) * g


def _dot(a, b):
    return jnp.dot(a, b, preferred_element_type=F32)


def _const_spec(shape):
    nd = len(shape)
    return pl.BlockSpec(shape, lambda *_: (0,) * nd, pipeline_mode=pl.Buffered(1))


def _ada_kernel(c_ref, w_ref, b_ref, o_ref):
    c = c_ref[...]
    s = c * jax.nn.sigmoid(c)
    o_ref[...] = jnp.dot(s, w_ref[...], precision=lax.Precision.HIGHEST,
                         preferred_element_type=F32) + b_ref[...]


def _ada(c_pad, w_ada, b_ada):
    n = w_ada.shape[1]
    bn = 1536
    return pl.pallas_call(
        _ada_kernel,
        grid=(n // bn,),
        in_specs=[pl.BlockSpec(c_pad.shape, lambda j: (0, 0)),
                  pl.BlockSpec((D_MODEL, bn), lambda j: (0, j)),
                  pl.BlockSpec((1, bn), lambda j: (0, j))],
        out_specs=pl.BlockSpec((c_pad.shape[0], bn), lambda j: (0, j)),
        out_shape=jax.ShapeDtypeStruct((c_pad.shape[0], n), F32),
        compiler_params=pltpu.CompilerParams(vmem_limit_bytes=VMEM_LIMIT),
        name="ada_mod",
    )(c_pad, w_ada, b_ada.reshape(1, n))


def _mix_kernel(x_ref, xp_ref, xn_ref, mod_ref, gpre_ref, win_ref, cw_ref, lng_ref, lnb_ref,
                wsp_ref, bsp_ref, ga_ref, gb_ref, wout_ref, gpost_ref, *rest, n_i):
    o_ref, zbuf = rest[-2:]
    i = pl.program_id(1)
    tm = x_ref.shape[1]
    sh1 = mod_ref[0, 0:1, :]
    sc1 = mod_ref[0, 1:2, :]
    gt1 = mod_ref[0, 2:3, :]
    gpre = gpre_ref[...]

    def prenorm(xv):
        return _rms(xv, gpre) * (1.0 + sc1) + sh1

    x = x_ref[0]
    h = jnp.concatenate([prenorm(x), prenorm(xp_ref[0]), prenorm(xn_ref[0])], axis=0).astype(BF16)
    ht = h[:tm]

    cz = _dot(h, win_ref[:, CONV_W:3 * CONV_W])
    z = cz[:, :CONV_W] * cz[:, CONV_W:]
    zbuf[HALO:HALO + tm, :] = z[:tm]
    zbuf[0:HALO, :] = jnp.where(i == 0, 0.0, z[tm:tm + HALO])
    zbuf[HALO + tm:, :] = jnp.where(i == n_i - 1, 0.0, z[tm + HALO:])
    cw = cw_ref[...]
    conv = (zbuf[HALO - 1:HALO - 1 + tm, :] * cw[0:1] + zbuf[HALO:HALO + tm, :] * cw[1:2]
            + zbuf[HALO + 1:HALO + 1 + tm, :] * cw[2:3])
    bg = _dot(ht, win_ref[:, :CONV_W])
    ya = _rms(bg * conv, ga_ref[...])

    uv = _dot(ht, win_ref[:, 3 * CONV_W:])
    u = jax.nn.gelu(uv[:, :SGU_W], approximate=True)
    v = jax.nn.gelu(uv[:, SGU_W:], approximate=True)
    lng = lng_ref[...]
    lnb = lnb_ref[...]
    nchunk = tm // CHUNK
    parts = []
    for hd in range(HEADS):
        lo, hi = hd * HEAD_DIM, (hd + 1) * HEAD_DIM
        vh = v[:, lo:hi]
        mu = jnp.mean(vh, axis=-1, keepdims=True)
        vc = vh - mu
        var = jnp.mean(vc * vc, axis=-1, keepdims=True)
        vn = vc * lax.rsqrt(var + EPS) * lng[:, lo:hi] + lnb[:, lo:hi]
        vcat = jnp.concatenate([vn[c * CHUNK:(c + 1) * CHUNK] for c in range(nchunk)],
                               axis=1).astype(BF16)
        sp = _dot(wsp_ref[hd], vcat) + bsp_ref[:, hd:hd + 1]
        sp_rows = jnp.concatenate([sp[:, c * HEAD_DIM:(c + 1) * HEAD_DIM] for c in range(nchunk)],
                                  axis=0)
        parts.append(u[:, lo:hi] * sp_rows)
    yb = _rms(jnp.concatenate(parts, axis=1), gb_ref[...])

    cat = jnp.concatenate([ya, yb], axis=1).astype(BF16)
    mix = _dot(cat, wout_ref[...])
    o_ref[0] = x + gt1 * _rms(mix, gpost_ref[...])


def _mix(x, mod, w, tm, b0, b_total, x1_prev=None):
    b, s, d = x.shape
    n_i = s // tm
    hb = tm // HALO
    n_hb = s // HALO
    row = lambda a: a.reshape(1, -1)
    extra_specs, extra_args, aliases = [], [], {}
    if x1_prev is not None:
        extra_specs, extra_args, aliases = [pl.BlockSpec(memory_space=pl.ANY)], [x1_prev], {15: 0}
    return pl.pallas_call(
        functools.partial(_mix_kernel, n_i=n_i),
        grid=(b, n_i),
        in_specs=[
            pl.BlockSpec((1, tm, d), lambda bb, i: (bb, i, 0)),
            pl.BlockSpec((1, HALO, d), lambda bb, i: (bb, jnp.maximum(i * hb - 1, 0), 0)),
            pl.BlockSpec((1, HALO, d), lambda bb, i: (bb, jnp.minimum((i + 1) * hb, n_hb - 1), 0)),
            pl.BlockSpec((1, 6, d), lambda bb, i: (b0 + bb, 0, 0)),
            _const_spec((1, d)),
            _const_spec(w["w_in"].shape),
            _const_spec((3, CONV_W)),
            _const_spec((1, SGU_W)),
            _const_spec((1, SGU_W)),
            _const_spec((HEADS, CHUNK, CHUNK)),
            _const_spec((CHUNK, HEADS)),
            _const_spec((1, CONV_W)),
            _const_spec((1, SGU_W)),
            _const_spec((d, d)),
            _const_spec((1, d)),
        ] + extra_specs,
        out_specs=pl.BlockSpec((1, tm, d), lambda bb, i: (b0 + bb, i, 0)),
        out_shape=jax.ShapeDtypeStruct((b_total, s, d), F32),
        input_output_aliases=aliases,
        scratch_shapes=[pltpu.VMEM((tm + 2 * HALO, CONV_W), F32)],
        compiler_params=pltpu.CompilerParams(
            dimension_semantics=("arbitrary", "arbitrary"), vmem_limit_bytes=VMEM_LIMIT),
        name="token_mix",
    )(x, x, x, mod, row(w["g_pre_mix"]), w["w_in"], w["conv_w"], row(w["ln_v_g"]), row(w["ln_v_b"]),
      w["w_spatial"], w["b_spatial_t"], row(w["g_out_a"]), row(w["g_out_b"]), w["w_out"],
      row(w["g_post_mix"]), *extra_args)


def _router_kernel(x_ref, mod_ref, g_ref, whi_ref, wlo_ref, br_ref,
                   idx_ref, pos_ref, gate_ref, cnt_ref, carry):
    first = jnp.logical_and(pl.program_id(0) == 0, pl.program_id(1) == 0)

    @pl.when(first)
    def _():
        carry[...] = jnp.zeros_like(carry)

    tm = x_ref.shape[1]
    sh2 = mod_ref[0, 3:4, :]
    sc2 = mod_ref[0, 4:5, :]
    h2 = _rms(x_ref[0], g_ref[...]) * (1.0 + sc2) + sh2

    h_hi = h2.astype(BF16)
    h_lo = (h2 - h_hi.astype(F32)).astype(BF16)
    w_hi = whi_ref[...]
    logits_t = _dot(h_hi, w_hi) + (_dot(h_lo, w_hi) + _dot(h_hi, wlo_ref[...]))
    logits = logits_t.T
    scores = jax.nn.sigmoid(logits)
    sel = scores + br_ref[...]

    ri = lax.broadcasted_iota(I32, (N_EXP, LANES), 0)
    e_map = (ri % N_GRP) * GRP_SZ + ri // N_GRP
    gi = lax.broadcasted_iota(I32, (N_GRP, LANES), 0)

    n_chunk = tm // LANES
    sel_chunks, idx_chunks, score_chunks = [], [], []
    for c in range(n_chunk):
        sel_c = sel[:, c * LANES:(c + 1) * LANES]
        sc_c = scores[:, c * LANES:(c + 1) * LANES]
        m1 = sel_c[0:N_GRP]
        m2 = jnp.full_like(m1, -jnp.inf)
        for j in range(1, GRP_SZ):
            vj = sel_c[j * N_GRP:(j + 1) * N_GRP]
            m2 = jnp.maximum(m2, jnp.minimum(m1, vj))
            m1 = jnp.maximum(m1, vj)
        gs = m1 + m2
        grank = jnp.zeros(gs.shape, I32)
        for g2 in range(N_GRP):
            rowv = gs[g2:g2 + 1, :]
            beats = jnp.logical_or(rowv > gs, jnp.logical_and(rowv == gs, g2 < gi))
            grank = grank + beats.astype(I32)
        gmask = grank < TOP_GRP
        m = jnp.concatenate(
            [jnp.where(gmask, sel_c[j * N_GRP:(j + 1) * N_GRP], -jnp.inf) for j in range(GRP_SZ)],
            axis=0)
        chosen = jnp.zeros(m.shape, jnp.bool_)
        idx_rows, score_rows = [], []
        for _ in range(TOP_K):
            mx = jnp.max(m, axis=0, keepdims=True)
            e_min = jnp.min(jnp.where(m == mx, e_map, N_EXP), axis=0, keepdims=True)
            pick = e_map == e_min
            chosen = jnp.logical_or(chosen, pick)
            m = jnp.where(pick, -jnp.inf, m)
            idx_rows.append(e_min)
            score_rows.append(jnp.sum(jnp.where(pick, sc_c, 0.0), axis=0, keepdims=True))
        sel_chunks.append(chosen.astype(F32))
        idx_chunks.append(jnp.concatenate(idx_rows, axis=0))
        score_chunks.append(jnp.concatenate(score_rows, axis=0))
    self32 = jnp.concatenate(sel_chunks, axis=1)
    idx_k = jnp.concatenate(idx_chunks, axis=1)
    score_k = jnp.concatenate(score_chunks, axis=1)
    wsum = jnp.sum(score_k, axis=0, keepdims=True)
    gate_ref[...] = score_k / (wsum + 1e-20) * ROUTED_SCALE
    idx_ref[...] = idx_k

    selb = self32.astype(BF16)
    tr = lax.broadcasted_iota(I32, (tm, tm), 0)
    tc = lax.broadcasted_iota(I32, (tm, tm), 1)
    upper = (tr <= tc).astype(BF16)
    incl = _dot(selb, upper)
    pos_full = incl - self32 + carry[...]
    carry[...] = carry[...] + incl[:, tm - 1:tm]

    pos_chunks = []
    for c in range(n_chunk):
        pos_c = pos_full[:, c * LANES:(c + 1) * LANES]
        rows = []
        for k in range(TOP_K):
            pick = e_map == idx_k[k:k + 1, c * LANES:(c + 1) * LANES]
            rows.append(jnp.sum(jnp.where(pick, pos_c, 0.0), axis=0, keepdims=True))
        pos_chunks.append(jnp.concatenate(rows, axis=0))
    pos_ref[...] = jnp.concatenate(pos_chunks, axis=1).astype(I32)
    cnt_ref[...] = jnp.broadcast_to(carry[...], cnt_ref.shape)


def _router(x1, mod, g_pre_ffn, w_hi, w_lo, br, tm):
    b, s, d = x1.shape
    n_i = s // tm
    t = b * s
    return pl.pallas_call(
        _router_kernel,
        grid=(b, n_i),
        in_specs=[
            pl.BlockSpec((1, tm, d), lambda bb, i: (bb, i, 0)),
            pl.BlockSpec((1, 6, d), lambda bb, i: (bb, 0, 0)),
            _const_spec((1, d)),
            _const_spec((d, N_EXP)),
            _const_spec((d, N_EXP)),
            _const_spec((N_EXP, 1)),
        ],
        out_specs=[
            pl.BlockSpec((TOP_K, tm), lambda bb, i: (0, bb * n_i + i)),
            pl.BlockSpec((TOP_K, tm), lambda bb, i: (0, bb * n_i + i)),
            pl.BlockSpec((TOP_K, tm), lambda bb, i: (0, bb * n_i + i)),
            pl.BlockSpec((N_EXP, LANES), lambda bb, i: (0, 0)),
        ],
        out_shape=[
            jax.ShapeDtypeStruct((TOP_K, t), I32),
            jax.ShapeDtypeStruct((TOP_K, t), I32),
            jax.ShapeDtypeStruct((TOP_K, t), F32),
            jax.ShapeDtypeStruct((N_EXP, LANES), F32),
        ],
        scratch_shapes=[pltpu.VMEM((N_EXP, 1), F32)],
        compiler_params=pltpu.CompilerParams(
            dimension_semantics=("arbitrary", "arbitrary"), vmem_limit_bytes=VMEM_LIMIT),
        name="router",
    )(x1, mod, g_pre_ffn.reshape(1, d), w_hi, w_lo, br)


def _dispatch_kernel(dest_ref, x_ref, mod_ref, g_ref, wg_ref, wu_ref, wd_ref, xs_ref, sh_ref,
                     rowbuf, sem, *, n_b, n_i):
    td = x_ref.shape[1]
    n = pl.program_id(0) * n_i + pl.program_id(1)
    total = n_b * n_i
    n_rows = td * TOP_K * ROW_SUB

    def drain(slot):
        pltpu.make_async_copy(xs_ref.at[pl.ds(0, n_rows)], xs_ref.at[pl.ds(0, n_rows)],
                              sem.at[slot]).wait()

    def step(slot):
        @pl.when(n >= 2)
        def _():
            drain(slot)

        sh2 = mod_ref[0, 3:4, :]
        sc2 = mod_ref[0, 4:5, :]
        h2 = _rms(x_ref[0], g_ref[...]) * (1.0 + sc2) + sh2
        for s in range(ROW_SUB):
            rowbuf[slot, pl.ds(s, td, stride=ROW_SUB), :] = h2[:, s * LANES:(s + 1) * LANES]

        def body(j, c):
            src = rowbuf.at[slot, pl.ds(j * ROW_SUB, ROW_SUB)]
            for k in range(TOP_K):
                dst = dest_ref[0, 0, j * TOP_K + k]
                pltpu.make_async_copy(src, xs_ref.at[pl.ds(dst * ROW_SUB, ROW_SUB)],
                                      sem.at[slot]).start()
            return c
        lax.fori_loop(0, td, body, 0, unroll=2)

        hb = h2.astype(BF16)
        g = _dot(hb, wg_ref[...])
        u = _dot(hb, wu_ref[...])
        sh_ref[0] = _dot((g * jax.nn.sigmoid(g) * u).astype(BF16), wd_ref[...])

        @pl.when(n == total - 1)
        def _():
            drain(slot)
            if total >= 2:
                drain(1 - slot)

    @pl.when(n % 2 == 0)
    def _():
        step(0)

    @pl.when(n % 2 == 1)
    def _():
        step(1)


def _dispatch(dest_flat, x1, mod, w, n_slots, td):
    b, s, d = x1.shape
    n_i = s // td
    dest3 = dest_flat.reshape(b * n_i, 1, td * TOP_K)
    return pl.pallas_call(
        functools.partial(_dispatch_kernel, n_b=b, n_i=n_i),
        grid=(b, n_i),
        in_specs=[pl.BlockSpec((1, 1, td * TOP_K), lambda bb, i: (bb * n_i + i, 0, 0),
                               memory_space=pltpu.SMEM),
                  pl.BlockSpec((1, td, d), lambda bb, i: (bb, i, 0)),
                  pl.BlockSpec((1, 6, d), lambda bb, i: (bb, 0, 0)),
                  _const_spec((1, d)),
                  _const_spec((d, FF)),
                  _const_spec((d, FF)),
                  _const_spec((FF, d))],
        out_specs=[pl.BlockSpec(memory_space=pl.ANY),
                   pl.BlockSpec((1, td, d), lambda bb, i: (bb, i, 0))],
        out_shape=[jax.ShapeDtypeStruct((n_slots * ROW_SUB, LANES), F32),
                   jax.ShapeDtypeStruct((b, s, d), F32)],
        scratch_shapes=[pltpu.VMEM((2, td * ROW_SUB, LANES), F32),
                        pltpu.SemaphoreType.DMA((2,))],
        compiler_params=pltpu.CompilerParams(
            dimension_semantics=("arbitrary", "arbitrary"), vmem_limit_bytes=VMEM_LIMIT),
        name="dispatch",
    )(dest3, x1, mod, w["g_pre_ffn"].reshape(1, d), w["w_sh_gate"], w["w_sh_up"], w["w_sh_down"])


def _expert_kernel(be_ref, nv_ref, nu_ref, xs_ref, wg_ref, wu_ref, wd_ref, ys_ref,
                   wg_s, wu_s, wd_s):
    i = pl.program_id(0)
    tme = xs_ref.shape[0] // ROW_SUB

    @pl.when(i < nu_ref[0])
    def _():
        prev = be_ref[jnp.maximum(i - 1, 0)]
        changed = jnp.logical_or(i == 0, prev != be_ref[i])

        @pl.when(changed)
        def _():
            wg_s[...] = wg_ref[0].astype(BF16)
            wu_s[...] = wu_ref[0].astype(BF16)
            wd_s[...] = wd_ref[0].astype(BF16)

        nv = nv_ref[i]
        sub = tme // EXPERT_SUB_BLOCKS
        for q in range(EXPERT_SUB_BLOCKS):
            @pl.when(q * sub < nv)
            def _():
                r0 = q * sub * ROW_SUB
                x = jnp.concatenate(
                    [xs_ref[pl.ds(r0 + s, sub, stride=ROW_SUB), :] for s in range(ROW_SUB)], axis=1)
                rows = q * sub + lax.broadcasted_iota(I32, (sub, 1), 0)
                x = jnp.where(rows < nv, x, 0.0).astype(BF16)
                g = _dot(x, wg_s[...])
                u = _dot(x, wu_s[...])
                a = (g * jax.nn.sigmoid(g) * u).astype(BF16)
                y = _dot(a, wd_s[...])
                for s in range(ROW_SUB):
                    ys_ref[pl.ds(r0 + s, sub, stride=ROW_SUB), :] = y[:, s * LANES:(s + 1) * LANES]


def _experts(block_e, nvalid, nused, xs, wg, wu, wd, tme):
    n_blocks = xs.shape[0] // (tme * ROW_SUB)
    blk = lambda i, be, nv, nu: (jnp.minimum(i, nu[0] - 1), 0)
    wsel = lambda i, be, nv, nu: (be[i], 0, 0)
    return pl.pallas_call(
        _expert_kernel,
        grid_spec=pltpu.PrefetchScalarGridSpec(
            num_scalar_prefetch=3,
            grid=(n_blocks,),
            in_specs=[pl.BlockSpec((tme * ROW_SUB, LANES), blk),
                      pl.BlockSpec((1, D_MODEL, FF), wsel),
                      pl.BlockSpec((1, D_MODEL, FF), wsel),
                      pl.BlockSpec((1, FF, D_MODEL), wsel)],
            out_specs=pl.BlockSpec((tme * ROW_SUB, LANES), blk),
            scratch_shapes=[pltpu.VMEM((D_MODEL, FF), BF16),
                            pltpu.VMEM((D_MODEL, FF), BF16),
                            pltpu.VMEM((FF, D_MODEL), BF16)],
        ),
        out_shape=jax.ShapeDtypeStruct(xs.shape, F32),
        compiler_params=pltpu.CompilerParams(
            dimension_semantics=("arbitrary",), vmem_limit_bytes=VMEM_LIMIT),
        name="experts",
    )(block_e, nvalid, nused, xs, wg, wu, wd)


N_GATHER_BUF = 3


def _combine_kernel(dest_ref, dest1_ref, dest2_ref, gexp_ref, x_ref, sh_ref, mod_ref, gpost_ref,
                    ys_ref, o_ref, buf, rsum, sem, *, n_b, n_i):
    tc = x_ref.shape[1]
    n = pl.program_id(0) * n_i + pl.program_id(1)
    total = n_b * n_i
    n_rows = tc * TOP_K * ROW_SUB
    half = ROW_SUB // 2

    def gather_rows(idx_ref, slot, unroll):
        def body(j, c):
            for k in range(TOP_K):
                src = idx_ref[0, 0, j * TOP_K + k]
                pltpu.make_async_copy(ys_ref.at[pl.ds(src * ROW_SUB, ROW_SUB)],
                                      buf.at[slot, pl.ds((j * TOP_K + k) * ROW_SUB, ROW_SUB)],
                                      sem.at[slot]).start()
            return c
        lax.fori_loop(0, tc, body, 0, unroll=unroll)

    def drain(slot):
        pltpu.make_async_copy(ys_ref.at[pl.ds(0, n_rows)], buf.at[slot], sem.at[slot]).wait()

    def step(slot):
        slot1 = (slot + 1) % N_GATHER_BUF
        slot2 = (slot + 2) % N_GATHER_BUF

        @pl.when(n == 0)
        def _():
            gather_rows(dest_ref, slot, 2)
            gather_rows(dest1_ref, slot1, 2)

        drain(slot)
        gather_rows(dest2_ref, slot2, True)

        for j in range(tc):
            base = j * (TOP_K * ROW_SUB)
            gk = jnp.broadcast_to(gexp_ref[pl.ds(j * TOP_K, 1), :], (half, LANES))
            acc0 = gk * buf[slot, pl.ds(base, half), :]
            acc1 = gk * buf[slot, pl.ds(base + half, half), :]
            for k in range(1, TOP_K):
                gk = jnp.broadcast_to(gexp_ref[pl.ds(j * TOP_K + k, 1), :], (half, LANES))
                acc0 = acc0 + gk * buf[slot, pl.ds(base + k * ROW_SUB, half), :]
                acc1 = acc1 + gk * buf[slot, pl.ds(base + k * ROW_SUB + half, half), :]
            rsum[pl.ds(j * ROW_SUB, half), :] = acc0
            rsum[pl.ds(j * ROW_SUB + half, half), :] = acc1

        routed = jnp.concatenate([rsum[pl.ds(s, tc, stride=ROW_SUB), :] for s in range(ROW_SUB)],
                                 axis=1)
        ffn = routed + sh_ref[0]
        gt2 = mod_ref[0, 5:6, :]
        o_ref[0] = x_ref[0] + gt2 * _rms(ffn, gpost_ref[...])

        @pl.when(n + 1 == total)
        def _():
            drain(slot1)
            drain(slot2)

    for slot in range(N_GATHER_BUF):
        pl.when(n % N_GATHER_BUF == slot)(functools.partial(step, slot))


def _combine(dest_flat, gates_rows,b0, b, x1, shared, mod, w, ys, tc):
    _, s, d = x1.shape
    n_i = s // tc
    t_all = dest_flat.shape[0] // TOP_K
    dest3 = dest_flat.reshape(t_all // tc, 1, tc * TOP_K)
    t0 = b0 * n_i
    last = t0 + b * n_i - 1
    cur = lambda bb, i: (t0 + bb * n_i + i, 0, 0)
    nxt1 = lambda bb, i: (jnp.minimum(t0 + bb * n_i + i + 1, last), 0, 0)
    nxt2 = lambda bb, i: (jnp.minimum(t0 + bb * n_i + i + 2, last), 0, 0)
    return pl.pallas_call(
        functools.partial(_combine_kernel, n_b=b, n_i=n_i),
        grid=(b, n_i),
        in_specs=[
            pl.BlockSpec((1, 1, tc * TOP_K), cur, memory_space=pltpu.SMEM),
            pl.BlockSpec((1, 1, tc * TOP_K), nxt1, memory_space=pltpu.SMEM),
            pl.BlockSpec((1, 1, tc * TOP_K), nxt2, memory_space=pltpu.SMEM),
            pl.BlockSpec((tc * TOP_K, LANES), lambda bb, i: (t0 + bb * n_i + i, 0)),
            pl.BlockSpec((1, tc, d), lambda bb, i: (b0 + bb, i, 0)),
            pl.BlockSpec((1, tc, d), lambda bb, i: (b0 + bb, i, 0)),
            pl.BlockSpec((1, 6, d), lambda bb, i: (b0 + bb, 0, 0)),
            _const_spec((1, d)),
            pl.BlockSpec(memory_space=pl.ANY),
        ],
        out_specs=pl.BlockSpec((1, tc, d), lambda bb, i: (bb, i, 0)),
        out_shape=jax.ShapeDtypeStruct((b, s, d), F32),
        scratch_shapes=[pltpu.VMEM((N_GATHER_BUF, tc * TOP_K * ROW_SUB, LANES), F32),
                        pltpu.VMEM((tc * ROW_SUB, LANES), F32),
                        pltpu.SemaphoreType.DMA((N_GATHER_BUF,))],
        compiler_params=pltpu.CompilerParams(
            dimension_semantics=("arbitrary", "arbitrary"), vmem_limit_bytes=VMEM_LIMIT),
        name="combine",
    )(dest3, dest3, dest3, gates_rows, x1, shared, mod, w["g_post_ffn"].reshape(1, d), ys)


def _tile_sizes(s):
    return dict(tm_mix=min(256, s), tm_route=min(512, s), td=min(256, s), tme=min(512, s),
                tc=min(128, s))


def _layer(x_prompt, x_sample, c_prompt, c_sample, lw):
    bp, s, d = x_prompt.shape
    bs = x_sample.shape[0]
    ts = _tile_sizes(s)

    nb = bp + bs
    nb_pad = -(-nb // 8) * 8
    c_all = jnp.concatenate([c_prompt, c_sample, jnp.zeros((nb_pad - nb, d), F32)], axis=0)
    mod = _ada(c_all, lw["w_ada"], lw["b_ada"])[:nb].reshape(nb, 6, d)

    w = dict(lw)
    w["w_in"] = lw["w_in"].astype(BF16)
    w["w_out"] = lw["w_out"].astype(BF16)
    w["w_spatial"] = lw["w_spatial"].astype(BF16)
    w["b_spatial_t"] = lw["b_spatial"].T
    w["w_sh_gate"] = lw["w_sh_gate"].astype(BF16)
    w["w_sh_up"] = lw["w_sh_up"].astype(BF16)
    w["w_sh_down"] = lw["w_sh_down"].astype(BF16)

    x1 = _mix(x_prompt, mod, w, ts["tm_mix"], 0, nb)
    x1 = _mix(x_sample, mod, w, ts["tm_mix"], bp, nb, x1_prev=x1)

    perm = (jnp.arange(N_EXP) % N_GRP) * GRP_SZ + jnp.arange(N_EXP) // N_GRP
    wr = lw["w_router"][:, perm]
    wr_hi = wr.astype(BF16)
    wr_lo = (wr - wr_hi.astype(F32)).astype(BF16)
    br = lw["router_bias"][perm].reshape(N_EXP, 1)
    idx, pos, gates, cnt_all = _router(x1, mod, lw["g_pre_ffn"], wr_hi, wr_lo, br, ts["tm_route"])

    tme = ts["tme"]
    t_all = nb * s
    counts = cnt_all[:, 0].astype(I32).reshape(GRP_SZ, N_GRP).T.reshape(N_EXP)
    padded = (counts + tme - 1) // tme * tme
    pad_end = jnp.cumsum(padded)
    off = pad_end - padded
    gates_rows = jnp.broadcast_to(gates.T.reshape(t_all * TOP_K, 1), (t_all * TOP_K, LANES))
    e_ids = jnp.arange(N_EXP, dtype=I32)
    off_of_idx = jnp.sum(jnp.where(idx[:, :, None] == e_ids, off, 0), axis=-1)
    dest_flat = (off_of_idx + pos).T.reshape(t_all * TOP_K)
    n_blocks = -(-t_all * TOP_K // tme) + N_EXP
    nused = (pad_end[-1] // tme).astype(I32)
    blk_ids = jnp.minimum(jnp.arange(n_blocks, dtype=I32), nused - 1)
    blk_start = blk_ids * tme
    block_e = jnp.minimum(jnp.sum((pad_end[None, :] <= blk_start[:, None]).astype(I32), axis=1),
                          N_EXP - 1)
    be_hot = block_e[:, None] == e_ids
    cnt_of_blk = jnp.sum(jnp.where(be_hot, counts, 0), axis=1)
    off_of_blk = jnp.sum(jnp.where(be_hot, off, 0), axis=1)
    nvalid = jnp.clip(cnt_of_blk - (blk_start - off_of_blk), 0, tme).astype(I32)

    xs, shared = _dispatch(dest_flat, x1, mod, w, n_blocks * tme, ts["td"])
    ys = _experts(block_e, nvalid, nused.reshape(1), xs, lw["w_exp_gate"], lw["w_exp_up"],
                  lw["w_exp_down"], tme)

    y_p = _combine(dest_flat, gates_rows,0, bp, x1, shared, mod, w, ys, ts["tc"])
    y_s = _combine(dest_flat, gates_rows,bp, bs, x1, shared, mod, w, ys, ts["tc"])
    return y_p, y_s


def kernel(x_prompt, x_sample, c_prompt, c_sample, w_ada, b_ada, g_pre_mix, w_in, conv_w, ln_v_g, ln_v_b, w_spatial, b_spatial, g_out_a, g_out_b, w_out, g_post_mix, g_pre_ffn, w_router, router_bias, w_exp_gate, w_exp_up, w_exp_down, w_sh_gate, w_sh_up, w_sh_down, g_post_ffn):
    names = ("w_ada", "b_ada", "g_pre_mix", "w_in", "conv_w", "ln_v_g", "ln_v_b", "w_spatial",
             "b_spatial", "g_out_a", "g_out_b", "w_out", "g_post_mix", "g_pre_ffn", "w_router",
             "router_bias", "w_exp_gate", "w_exp_up", "w_exp_down", "w_sh_gate", "w_sh_up",
             "w_sh_down", "g_post_ffn")
    stacked = (w_ada, b_ada, g_pre_mix, w_in, conv_w, ln_v_g, ln_v_b, w_spatial, b_spatial, g_out_a,
               g_out_b, w_out, g_post_mix, g_pre_ffn, w_router, router_bias, w_exp_gate, w_exp_up,
               w_exp_down, w_sh_gate, w_sh_up, w_sh_down, g_post_ffn)
    y_p, y_s = x_prompt, x_sample
    for l in range(w_ada.shape[0]):
        lw = {n: a[l] for n, a in zip(names, stacked)}
        y_p, y_s = _layer(y_p, y_s, c_prompt, c_sample, lw)
    return (y_p, y_s)
```

```python
import functools

import jax
import jax.numpy as jnp
from jax import lax
from jax.experimental import pallas as pl
from jax.experimental.pallas import tpu as pltpu
from jax.experimental.pallas import tpu_sc as plsc

F32 = jnp.float32
BF16 = jnp.bfloat16
I32 = jnp.int32

D_MODEL = 2048
CONV_W = 1024
SGU_W = 1024
HEADS = 8
HEAD_DIM = 128
CHUNK = 128
N_EXP = 64
TOP_K = 8
N_GRP = 8
GRP_SZ = 8
TOP_GRP = 4
FF = 512
EPS = 1e-6
ROUTED_SCALE = 2.5
LANES = 128
ROW_SUB = D_MODEL // LANES
HALO = 8
EXPERT_SUB_BLOCKS = 4
K_CHUNKS = 4
N_GATHER_BUF = 3

VMEM_LIMIT = 56 * 1024 * 1024


def _rms(x, g):
    return x * lax.rsqrt(jnp.mean(x * x, axis=-1, keepdims=True) + EPS) * g


def _dot(a, b):
    return jnp.dot(a, b, preferred_element_type=F32)


def _const_spec(shape):
    nd = len(shape)
    return pl.BlockSpec(shape, lambda *_: (0,) * nd, pipeline_mode=pl.Buffered(1))


def _ada_kernel(c_ref, w_ref, b_ref, o_ref):
    c = c_ref[...]
    s = c * jax.nn.sigmoid(c)
    o_ref[...] = jnp.dot(s, w_ref[...], precision=lax.Precision.HIGHEST,
                         preferred_element_type=F32) + b_ref[...]


def _ada(c_pad, w_ada, b_ada):
    n = w_ada.shape[1]
    bn = 1536
    return pl.pallas_call(
        _ada_kernel,
        grid=(n // bn,),
        in_specs=[pl.BlockSpec(c_pad.shape, lambda j: (0, 0)),
                  pl.BlockSpec((D_MODEL, bn), lambda j: (0, j)),
                  pl.BlockSpec((1, bn), lambda j: (0, j))],
        out_specs=pl.BlockSpec((c_pad.shape[0], bn), lambda j: (0, j)),
        out_shape=jax.ShapeDtypeStruct((c_pad.shape[0], n), F32),
        compiler_params=pltpu.CompilerParams(vmem_limit_bytes=VMEM_LIMIT),
        name="ada_mod",
    )(c_pad, w_ada, b_ada.reshape(1, n))


def _mix_kernel(x_ref, xp_ref, xn_ref, mod_ref, gpre_ref, win_ref, cw_ref, lng_ref, lnb_ref,
                wsp_ref, bsp_ref, ga_ref, gb_ref, wout_ref, gpost_ref, *rest, n_i):
    o_ref, zbuf = rest[-2:]
    i = pl.program_id(1)
    tm = x_ref.shape[1]
    sh1 = mod_ref[0, 0:1, :]
    sc1 = mod_ref[0, 1:2, :]
    gt1 = mod_ref[0, 2:3, :]
    gpre = gpre_ref[...]

    def prenorm(xv):
        return _rms(xv, gpre) * (1.0 + sc1) + sh1

    x = x_ref[0]
    h = jnp.concatenate([prenorm(x), prenorm(xp_ref[0]), prenorm(xn_ref[0])], axis=0).astype(BF16)
    ht = h[:tm]

    cz = _dot(h, win_ref[:, CONV_W:3 * CONV_W])
    z = cz[:, :CONV_W] * cz[:, CONV_W:]
    zbuf[HALO:HALO + tm, :] = z[:tm]
    zbuf[0:HALO, :] = jnp.where(i == 0, 0.0, z[tm:tm + HALO])
    zbuf[HALO + tm:, :] = jnp.where(i == n_i - 1, 0.0, z[tm + HALO:])
    cw = cw_ref[...]
    conv = (zbuf[HALO - 1:HALO - 1 + tm, :] * cw[0:1] + zbuf[HALO:HALO + tm, :] * cw[1:2]
            + zbuf[HALO + 1:HALO + 1 + tm, :] * cw[2:3])
    bg = _dot(ht, win_ref[:, :CONV_W])
    ya = _rms(bg * conv, ga_ref[...])

    uv = _dot(ht, win_ref[:, 3 * CONV_W:])
    u = jax.nn.gelu(uv[:, :SGU_W], approximate=True)
    v = jax.nn.gelu(uv[:, SGU_W:], approximate=True)
    lng = lng_ref[...]
    lnb = lnb_ref[...]
    nchunk = tm // CHUNK
    parts = []
    for hd in range(HEADS):
        lo, hi = hd * HEAD_DIM, (hd + 1) * HEAD_DIM
        vh = v[:, lo:hi]
        mu = jnp.mean(vh, axis=-1, keepdims=True)
        vc = vh - mu
        var = jnp.mean(vc * vc, axis=-1, keepdims=True)
        vn = vc * lax.rsqrt(var + EPS) * lng[:, lo:hi] + lnb[:, lo:hi]
        vcat = jnp.concatenate([vn[c * CHUNK:(c + 1) * CHUNK] for c in range(nchunk)],
                               axis=1).astype(BF16)
        sp = _dot(wsp_ref[hd], vcat) + bsp_ref[:, hd:hd + 1]
        sp_rows = jnp.concatenate([sp[:, c * HEAD_DIM:(c + 1) * HEAD_DIM] for c in range(nchunk)],
                                  axis=0)
        parts.append(u[:, lo:hi] * sp_rows)
    yb = _rms(jnp.concatenate(parts, axis=1), gb_ref[...])

    cat = jnp.concatenate([ya, yb], axis=1).astype(BF16)
    mix = _dot(cat, wout_ref[...])
    o_ref[0] = x + gt1 * _rms(mix, gpost_ref[...])


def _mix(x, mod, w, tm, b0, b_total, x1_prev=None):
    b, s, d = x.shape
    n_i = s // tm
    hb = tm // HALO
    n_hb = s // HALO
    row = lambda a: a.reshape(1, -1)
    extra_specs, extra_args, aliases = [], [], {}
    if x1_prev is not None:
        extra_specs, extra_args, aliases = [pl.BlockSpec(memory_space=pl.ANY)], [x1_prev], {15: 0}
    return pl.pallas_call(
        functools.partial(_mix_kernel, n_i=n_i),
        grid=(b, n_i),
        in_specs=[
            pl.BlockSpec((1, tm, d), lambda bb, i: (bb, i, 0)),
            pl.BlockSpec((1, HALO, d), lambda bb, i: (bb, jnp.maximum(i * hb - 1, 0), 0)),
            pl.BlockSpec((1, HALO, d), lambda bb, i: (bb, jnp.minimum((i + 1) * hb, n_hb - 1), 0)),
            pl.BlockSpec((1, 6, d), lambda bb, i: (b0 + bb, 0, 0)),
            _const_spec((1, d)),
            _const_spec(w["w_in"].shape),
            _const_spec((3, CONV_W)),
            _const_spec((1, SGU_W)),
            _const_spec((1, SGU_W)),
            _const_spec((HEADS, CHUNK, CHUNK)),
            _const_spec((CHUNK, HEADS)),
            _const_spec((1, CONV_W)),
            _const_spec((1, SGU_W)),
            _const_spec((d, d)),
            _const_spec((1, d)),
        ] + extra_specs,
        out_specs=pl.BlockSpec((1, tm, d), lambda bb, i: (b0 + bb, i, 0)),
        out_shape=jax.ShapeDtypeStruct((b_total, s, d), F32),
        input_output_aliases=aliases,
        scratch_shapes=[pltpu.VMEM((tm + 2 * HALO, CONV_W), F32)],
        compiler_params=pltpu.CompilerParams(
            dimension_semantics=("arbitrary", "arbitrary"), vmem_limit_bytes=VMEM_LIMIT),
        name="token_mix",
    )(x, x, x, mod, row(w["g_pre_mix"]), w["w_in"], w["conv_w"], row(w["ln_v_g"]), row(w["ln_v_b"]),
      w["w_spatial"], w["b_spatial_t"], row(w["g_out_a"]), row(w["g_out_b"]), w["w_out"],
      row(w["g_post_mix"]), *extra_args)


def _router_kernel(x_ref, mod_ref, g_ref, whi_ref, wlo_ref, br_ref,
                   h2_ref, idx_ref, pos_ref, gate_ref, cnt_ref, carry):
    first = jnp.logical_and(pl.program_id(0) == 0, pl.program_id(1) == 0)

    @pl.when(first)
    def _():
        carry[...] = jnp.zeros_like(carry)

    tm = x_ref.shape[1]
    sh2 = mod_ref[0, 3:4, :]
    sc2 = mod_ref[0, 4:5, :]
    h2 = _rms(x_ref[0], g_ref[...]) * (1.0 + sc2) + sh2
    for s in range(ROW_SUB):
        h2_ref[pl.ds(s, tm, stride=ROW_SUB), :] = h2[:, s * LANES:(s + 1) * LANES]

    h_hi = h2.astype(BF16)
    h_lo = (h2 - h_hi.astype(F32)).astype(BF16)
    w_hi = whi_ref[...]
    logits_t = _dot(h_hi, w_hi) + (_dot(h_lo, w_hi) + _dot(h_hi, wlo_ref[...]))
    logits = logits_t.T
    scores = jax.nn.sigmoid(logits)
    sel = scores + br_ref[...]

    ri = lax.broadcasted_iota(I32, (N_EXP, LANES), 0)
    e_map = (ri % N_GRP) * GRP_SZ + ri // N_GRP
    gi = lax.broadcasted_iota(I32, (N_GRP, LANES), 0)

    n_chunk = tm // LANES
    sel_chunks, idx_chunks, score_chunks = [], [], []
    for c in range(n_chunk):
        sel_c = sel[:, c * LANES:(c + 1) * LANES]
        sc_c = scores[:, c * LANES:(c + 1) * LANES]
        m1 = sel_c[0:N_GRP]
        m2 = jnp.full_like(m1, -jnp.inf)
        for j in range(1, GRP_SZ):
            vj = sel_c[j * N_GRP:(j + 1) * N_GRP]
            m2 = jnp.maximum(m2, jnp.minimum(m1, vj))
            m1 = jnp.maximum(m1, vj)
        gs = m1 + m2
        grank = jnp.zeros(gs.shape, I32)
        for g2 in range(N_GRP):
            rowv = gs[g2:g2 + 1, :]
            beats = jnp.logical_or(rowv > gs, jnp.logical_and(rowv == gs, g2 < gi))
            grank = grank + beats.astype(I32)
        gmask = grank < TOP_GRP
        m = jnp.concatenate(
            [jnp.where(gmask, sel_c[j * N_GRP:(j + 1) * N_GRP], -jnp.inf) for j in range(GRP_SZ)],
            axis=0)
        chosen = jnp.zeros(m.shape, jnp.bool_)
        idx_rows, score_rows = [], []
        for _ in range(TOP_K):
            mx = jnp.max(m, axis=0, keepdims=True)
            e_min = jnp.min(jnp.where(m == mx, e_map, N_EXP), axis=0, keepdims=True)
            pick = e_map == e_min
            chosen = jnp.logical_or(chosen, pick)
            m = jnp.where(pick, -jnp.inf, m)
            idx_rows.append(e_min)
            score_rows.append(jnp.sum(jnp.where(pick, sc_c, 0.0), axis=0, keepdims=True))
        sel_chunks.append(chosen.astype(F32))
        idx_chunks.append(jnp.concatenate(idx_rows, axis=0))
        score_chunks.append(jnp.concatenate(score_rows, axis=0))
    self32 = jnp.concatenate(sel_chunks, axis=1)
    idx_k = jnp.concatenate(idx_chunks, axis=1)
    score_k = jnp.concatenate(score_chunks, axis=1)
    wsum = jnp.sum(score_k, axis=0, keepdims=True)
    gate_ref[...] = score_k / (wsum + 1e-20) * ROUTED_SCALE
    idx_ref[...] = idx_k

    selb = self32.astype(BF16)
    tr = lax.broadcasted_iota(I32, (tm, tm), 0)
    tc = lax.broadcasted_iota(I32, (tm, tm), 1)
    upper = (tr <= tc).astype(BF16)
    incl = _dot(selb, upper)
    pos_full = incl - self32 + carry[...]
    carry[...] = carry[...] + incl[:, tm - 1:tm]

    pos_chunks = []
    for c in range(n_chunk):
        pos_c = pos_full[:, c * LANES:(c + 1) * LANES]
        rows = []
        for k in range(TOP_K):
            pick = e_map == idx_k[k:k + 1, c * LANES:(c + 1) * LANES]
            rows.append(jnp.sum(jnp.where(pick, pos_c, 0.0), axis=0, keepdims=True))
        pos_chunks.append(jnp.concatenate(rows, axis=0))
    pos_ref[...] = jnp.concatenate(pos_chunks, axis=1).astype(I32)
    cnt_ref[...] = jnp.broadcast_to(carry[...], cnt_ref.shape)


def _router(x1, mod, w, w_hi, w_lo, br, tm):
    b, s, d = x1.shape
    n_i = s // tm
    t = b * s
    return pl.pallas_call(
        _router_kernel,
        grid=(b, n_i),
        in_specs=[
            pl.BlockSpec((1, tm, d), lambda bb, i: (bb, i, 0)),
            pl.BlockSpec((1, 6, d), lambda bb, i: (bb, 0, 0)),
            _const_spec((1, d)),
            _const_spec((d, N_EXP)),
            _const_spec((d, N_EXP)),
            _const_spec((N_EXP, 1)),
        ],
        out_specs=[
            pl.BlockSpec((tm * ROW_SUB, LANES), lambda bb, i: (bb * n_i + i, 0)),
            pl.BlockSpec((TOP_K, tm), lambda bb, i: (0, bb * n_i + i)),
            pl.BlockSpec((TOP_K, tm), lambda bb, i: (0, bb * n_i + i)),
            pl.BlockSpec((TOP_K, tm), lambda bb, i: (0, bb * n_i + i)),
            pl.BlockSpec((N_EXP, LANES), lambda bb, i: (0, 0)),
        ],
        out_shape=[
            jax.ShapeDtypeStruct((t * ROW_SUB, LANES), F32),
            jax.ShapeDtypeStruct((TOP_K, t), I32),
            jax.ShapeDtypeStruct((TOP_K, t), I32),
            jax.ShapeDtypeStruct((TOP_K, t), F32),
            jax.ShapeDtypeStruct((N_EXP, LANES), F32),
        ],
        scratch_shapes=[pltpu.VMEM((N_EXP, 1), F32)],
        compiler_params=pltpu.CompilerParams(
            dimension_semantics=("arbitrary", "arbitrary"), vmem_limit_bytes=VMEM_LIMIT),
        name="router",
    )(x1, mod, w["g_pre_ffn"].reshape(1, d), w_hi, w_lo, br)


def _slot_tokens(dest_flat, n_slots):
    info = plsc.get_sparse_core_info()
    n_workers = info.num_cores * info.num_subcores
    lanes = info.num_lanes
    n_pairs = dest_flat.shape[0]
    per_worker = n_slots // n_workers
    chunk = min(n_pairs, 16384)
    pair_shift = TOP_K.bit_length() - 1
    assert n_slots % (n_workers * lanes) == 0 and n_pairs % chunk == 0 and chunk % lanes == 0
    mesh = plsc.VectorSubcoreMesh(core_axis_name="core", subcore_axis_name="subcore")

    def body(dest_hbm, tok_hbm, dbuf, obuf):
        wid = lax.axis_index("subcore") * info.num_cores + lax.axis_index("core")
        lo = wid * per_worker

        @pl.loop(0, per_worker, step=lanes)
        def _(r):
            obuf[pl.ds(r, lanes)] = jnp.zeros((lanes,), I32)

        @pl.loop(0, n_pairs, step=chunk)
        def _(c0):
            pltpu.sync_copy(dest_hbm.at[pl.ds(c0, chunk)], dbuf)

            @pl.loop(0, chunk, step=lanes)
            def _(j):
                loc = dbuf[pl.ds(j, lanes)] - lo
                mine = jnp.logical_and(loc >= 0, loc < per_worker)
                tok = lax.shift_right_logical(c0 + j + lax.iota(I32, lanes), pair_shift)
                plsc.store_scatter(obuf, [jnp.where(mine, loc, 0)], tok, mask=mine)

        pltpu.sync_copy(obuf, tok_hbm.at[pl.ds(lo, per_worker)])

    return pl.kernel(
        body, out_type=jax.ShapeDtypeStruct((n_slots,), I32), mesh=mesh,
        scratch_types=[pltpu.VMEM((chunk,), I32), pltpu.VMEM((per_worker,), I32)],
        compiler_params=pltpu.CompilerParams(needs_layout_passes=False),
        name="slot_tokens",
    )(dest_flat)


def _expert_kernel(be_ref, nu_ref, tok_ref, tokn_ref, h2_ref, wg_ref, wu_ref, wd_ref, ys_ref,
                   xbuf, wg_s, wu_s, wd_s, gsem):
    i = pl.program_id(0)
    nu = nu_ref[0]
    n_rows = xbuf.shape[1]
    tme = n_rows // ROW_SUB

    def request(idx_ref, slot, r):
        tok = idx_ref[0, 0, r]
        pltpu.make_async_copy(h2_ref.at[pl.ds(tok * ROW_SUB, ROW_SUB)],
                              xbuf.at[slot, pl.ds(r * ROW_SUB, ROW_SUB)], gsem.at[slot]).start()

    def drain(slot):
        pltpu.make_async_copy(h2_ref.at[pl.ds(0, n_rows)], xbuf.at[slot], gsem.at[slot]).wait()

    def step(slot):
        @pl.when(i == 0)
        def _():
            def body(r, c):
                request(tok_ref, slot, r)
                return c
            lax.fori_loop(0, tme, body, 0, unroll=8)

        prev = be_ref[jnp.maximum(i - 1, 0)]
        changed = jnp.logical_or(i == 0, prev != be_ref[i])

        @pl.when(changed)
        def _():
            wg_s[...] = wg_ref[0].astype(BF16)
            wu_s[...] = wu_ref[0].astype(BF16)
            wd_s[...] = wd_ref[0].astype(BF16)

        drain(slot)

        sub = tme // EXPERT_SUB_BLOCKS
        s_per = ROW_SUB // K_CHUNKS
        kc = D_MODEL // K_CHUNKS
        per_piece = sub // K_CHUNKS
        for q in range(EXPERT_SUB_BLOCKS):
            r0 = q * sub * ROW_SUB
            g = None
            u = None
            for c in range(K_CHUNKS):
                xc = jnp.concatenate(
                    [xbuf[slot, pl.ds(r0 + s, sub, stride=ROW_SUB), :]
                     for s in range(c * s_per, (c + 1) * s_per)], axis=1).astype(BF16)
                for r in range(q * sub + c * per_piece, q * sub + (c + 1) * per_piece):
                    request(tokn_ref, 1 - slot, r)
                gc = _dot(xc, wg_s[c * kc:(c + 1) * kc, :])
                uc = _dot(xc, wu_s[c * kc:(c + 1) * kc, :])
                g = gc if g is None else g + gc
                u = uc if u is None else u + uc
            a = (g * jax.nn.sigmoid(g) * u).astype(BF16)
            y = _dot(a, wd_s[...])
            for s in range(ROW_SUB):
                ys_ref[pl.ds(r0 + s, sub, stride=ROW_SUB), :] = y[:, s * LANES:(s + 1) * LANES]

        @pl.when(i == nu - 1)
        def _():
            drain(1 - slot)

    @pl.when(jnp.logical_and(i < nu, i % 2 == 0))
    def _():
        step(0)

    @pl.when(jnp.logical_and(i < nu, i % 2 == 1))
    def _():
        step(1)


def _experts(block_e, nused, slot_tok, h2rows, wg, wu, wd, tme):
    n_blocks = slot_tok.shape[0] // tme
    tok3 = slot_tok.reshape(n_blocks, 1, tme)
    cur3 = lambda i, be, nu: (jnp.minimum(i, nu[0] - 1), 0, 0)
    nxt3 = lambda i, be, nu: (jnp.minimum(i + 1, nu[0] - 1), 0, 0)
    cur2 = lambda i, be, nu: (jnp.minimum(i, nu[0] - 1), 0)
    wsel = lambda i, be, nu: (be[i], 0, 0)
    return pl.pallas_call(
        _expert_kernel,
        grid_spec=pltpu.PrefetchScalarGridSpec(
            num_scalar_prefetch=2,
            grid=(n_blocks,),
            in_specs=[pl.BlockSpec((1, 1, tme), cur3, memory_space=pltpu.SMEM),
                      pl.BlockSpec((1, 1, tme), nxt3, memory_space=pltpu.SMEM),
                      pl.BlockSpec(memory_space=pl.ANY),
                      pl.BlockSpec((1, D_MODEL, FF), wsel),
                      pl.BlockSpec((1, D_MODEL, FF), wsel),
                      pl.BlockSpec((1, FF, D_MODEL), wsel)],
            out_specs=pl.BlockSpec((tme * ROW_SUB, LANES), cur2),
            scratch_shapes=[pltpu.VMEM((2, tme * ROW_SUB, LANES), F32),
                            pltpu.VMEM((D_MODEL, FF), BF16),
                            pltpu.VMEM((D_MODEL, FF), BF16),
                            pltpu.VMEM((FF, D_MODEL), BF16),
                            pltpu.SemaphoreType.DMA((2,))],
        ),
        out_shape=jax.ShapeDtypeStruct((n_blocks * tme * ROW_SUB, LANES), F32),
        compiler_params=pltpu.CompilerParams(
            dimension_semantics=("arbitrary",), vmem_limit_bytes=VMEM_LIMIT),
        name="experts",
    )(block_e, nused, tok3, tok3, h2rows, wg, wu, wd)


def _combine_kernel(dest_ref, dest1_ref, dest2_ref, gexp_ref, x_ref, mod_ref, gffn_ref, gpost_ref,
                    wg_ref, wu_ref, wd_ref, ys_ref, o_ref, buf, rsum, sem, *, n_b, n_i):
    tc = x_ref.shape[1]
    n = pl.program_id(0) * n_i + pl.program_id(1)
    total = n_b * n_i
    n_rows = tc * TOP_K * ROW_SUB
    half = ROW_SUB // 2

    def gather_rows(idx_ref, slot, unroll):
        def body(j, c):
            for k in range(TOP_K):
                src = idx_ref[0, 0, j * TOP_K + k]
                pltpu.make_async_copy(ys_ref.at[pl.ds(src * ROW_SUB, ROW_SUB)],
                                      buf.at[slot, pl.ds((j * TOP_K + k) * ROW_SUB, ROW_SUB)],
                                      sem.at[slot]).start()
            return c
        lax.fori_loop(0, tc, body, 0, unroll=unroll)

    def drain(slot):
        pltpu.make_async_copy(ys_ref.at[pl.ds(0, n_rows)], buf.at[slot], sem.at[slot]).wait()

    def step(slot):
        slot1 = (slot + 1) % N_GATHER_BUF
        slot2 = (slot + 2) % N_GATHER_BUF

        @pl.when(n == 0)
        def _():
            gather_rows(dest_ref, slot, 2)
            gather_rows(dest1_ref, slot1, 2)

        drain(slot)
        gather_rows(dest2_ref, slot2, True)

        sh2 = mod_ref[0, 3:4, :]
        sc2 = mod_ref[0, 4:5, :]
        x1 = x_ref[0]
        hb = (_rms(x1, gffn_ref[...]) * (1.0 + sc2) + sh2).astype(BF16)
        sg = _dot(hb, wg_ref[...])
        su = _dot(hb, wu_ref[...])
        shared = _dot((sg * jax.nn.sigmoid(sg) * su).astype(BF16), wd_ref[...])

        for j in range(tc):
            base = j * (TOP_K * ROW_SUB)
            gk = jnp.broadcast_to(gexp_ref[pl.ds(j * TOP_K, 1), :], (half, LANES))
            acc0 = gk * buf[slot, pl.ds(base, half), :]
            acc1 = gk * buf[slot, pl.ds(base + half, half), :]
            for k in range(1, TOP_K):
                gk = jnp.broadcast_to(gexp_ref[pl.ds(j * TOP_K + k, 1), :], (half, LANES))
                acc0 = acc0 + gk * buf[slot, pl.ds(base + k * ROW_SUB, half), :]
                acc1 = acc1 + gk * buf[slot, pl.ds(base + k * ROW_SUB + half, half), :]
            rsum[pl.ds(j * ROW_SUB, half), :] = acc0
            rsum[pl.ds(j * ROW_SUB + half, half), :] = acc1

        routed = jnp.concatenate([rsum[pl.ds(s, tc, stride=ROW_SUB), :] for s in range(ROW_SUB)],
                                 axis=1)
        ffn = routed + shared
        gt2 = mod_ref[0, 5:6, :]
        o_ref[0] = x1 + gt2 * _rms(ffn, gpost_ref[...])

        @pl.when(n + 1 == total)
        def _():
            drain(slot1)
            drain(slot2)

    for slot in range(N_GATHER_BUF):
        pl.when(n % N_GATHER_BUF == slot)(functools.partial(step, slot))


def _combine(dest_flat, gates_rows, b0, b, x1, mod, w, ys, tc):
    _, s, d = x1.shape
    n_i = s // tc
    t_all = dest_flat.shape[0] // TOP_K
    dest3 = dest_flat.reshape(t_all // tc, 1, tc * TOP_K)
    t0 = b0 * n_i
    last = t0 + b * n_i - 1
    cur = lambda bb, i: (t0 + bb * n_i + i, 0, 0)
    nxt1 = lambda bb, i: (jnp.minimum(t0 + bb * n_i + i + 1, last), 0, 0)
    nxt2 = lambda bb, i: (jnp.minimum(t0 + bb * n_i + i + 2, last), 0, 0)
    return pl.pallas_call(
        functools.partial(_combine_kernel, n_b=b, n_i=n_i),
        grid=(b, n_i),
        in_specs=[
            pl.BlockSpec((1, 1, tc * TOP_K), cur, memory_space=pltpu.SMEM),
            pl.BlockSpec((1, 1, tc * TOP_K), nxt1, memory_space=pltpu.SMEM),
            pl.BlockSpec((1, 1, tc * TOP_K), nxt2, memory_space=pltpu.SMEM),
            pl.BlockSpec((tc * TOP_K, LANES), lambda bb, i: (t0 + bb * n_i + i, 0)),
            pl.BlockSpec((1, tc, d), lambda bb, i: (b0 + bb, i, 0)),
            pl.BlockSpec((1, 6, d), lambda bb, i: (b0 + bb, 0, 0)),
            _const_spec((1, d)),
            _const_spec((1, d)),
            _const_spec((d, FF)),
            _const_spec((d, FF)),
            _const_spec((FF, d)),
            pl.BlockSpec(memory_space=pl.ANY),
        ],
        out_specs=pl.BlockSpec((1, tc, d), lambda bb, i: (bb, i, 0)),
        out_shape=jax.ShapeDtypeStruct((b, s, d), F32),
        scratch_shapes=[pltpu.VMEM((N_GATHER_BUF, tc * TOP_K * ROW_SUB, LANES), F32),
                        pltpu.VMEM((tc * ROW_SUB, LANES), F32),
                        pltpu.SemaphoreType.DMA((N_GATHER_BUF,))],
        compiler_params=pltpu.CompilerParams(
            dimension_semantics=("arbitrary", "arbitrary"), vmem_limit_bytes=VMEM_LIMIT),
        name="combine",
    )(dest3, dest3, dest3, gates_rows, x1, mod, w["g_pre_ffn"].reshape(1, d),
      w["g_post_ffn"].reshape(1, d), w["w_sh_gate"], w["w_sh_up"], w["w_sh_down"], ys)


def _tile_sizes(s):
    return dict(tm_mix=min(256, s), tm_route=min(512, s), tme=min(512, s), tc=min(128, s))


def _layer(x_prompt, x_sample, c_prompt, c_sample, lw):
    bp, s, d = x_prompt.shape
    bs = x_sample.shape[0]
    ts = _tile_sizes(s)

    nb = bp + bs
    nb_pad = -(-nb // 8) * 8
    c_all = jnp.concatenate([c_prompt, c_sample, jnp.zeros((nb_pad - nb, d), F32)], axis=0)
    mod = _ada(c_all, lw["w_ada"], lw["b_ada"])[:nb].reshape(nb, 6, d)

    w = dict(lw)
    w["w_in"] = lw["w_in"].astype(BF16)
    w["w_out"] = lw["w_out"].astype(BF16)
    w["w_spatial"] = lw["w_spatial"].astype(BF16)
    w["b_spatial_t"] = lw["b_spatial"].T
    w["w_sh_gate"] = lw["w_sh_gate"].astype(BF16)
    w["w_sh_up"] = lw["w_sh_up"].astype(BF16)
    w["w_sh_down"] = lw["w_sh_down"].astype(BF16)

    x1 = _mix(x_prompt, mod, w, ts["tm_mix"], 0, nb)
    x1 = _mix(x_sample, mod, w, ts["tm_mix"], bp, nb, x1_prev=x1)

    perm = (jnp.arange(N_EXP) % N_GRP) * GRP_SZ + jnp.arange(N_EXP) // N_GRP
    wr = lw["w_router"][:, perm]
    wr_hi = wr.astype(BF16)
    wr_lo = (wr - wr_hi.astype(F32)).astype(BF16)
    br = lw["router_bias"][perm].reshape(N_EXP, 1)
    h2rows, idx, pos, gates, cnt_all = _router(x1, mod, w, wr_hi, wr_lo, br, ts["tm_route"])

    tme = ts["tme"]
    t_all = nb * s
    counts = cnt_all[:, 0].astype(I32).reshape(GRP_SZ, N_GRP).T.reshape(N_EXP)
    padded = (counts + tme - 1) // tme * tme
    pad_end = jnp.cumsum(padded)
    off = pad_end - padded
    gates_rows = jnp.broadcast_to(gates.T.reshape(t_all * TOP_K, 1), (t_all * TOP_K, LANES))
    e_ids = jnp.arange(N_EXP, dtype=I32)
    off_of_idx = jnp.sum(jnp.where(idx[:, :, None] == e_ids, off, 0), axis=-1)
    dest_flat = (off_of_idx + pos).T.reshape(t_all * TOP_K)
    n_blocks = -(-t_all * TOP_K // tme) + N_EXP
    nused = (pad_end[-1] // tme).astype(I32)
    blk_ids = jnp.minimum(jnp.arange(n_blocks, dtype=I32), nused - 1)
    block_e = jnp.minimum(
        jnp.sum((pad_end[None, :] <= (blk_ids * tme)[:, None]).astype(I32), axis=1), N_EXP - 1)

    slot_tok = _slot_tokens(dest_flat, n_blocks * tme)
    ys = _experts(block_e, nused.reshape(1), slot_tok, h2rows, lw["w_exp_gate"], lw["w_exp_up"],
                  lw["w_exp_down"], tme)

    y_p = _combine(dest_flat, gates_rows, 0, bp, x1, mod, w, ys, ts["tc"])
    y_s = _combine(dest_flat, gates_rows, bp, bs, x1, mod, w, ys, ts["tc"])
    return y_p, y_s


def kernel(x_prompt, x_sample, c_prompt, c_sample, w_ada, b_ada, g_pre_mix, w_in, conv_w, ln_v_g, ln_v_b, w_spatial, b_spatial, g_out_a, g_out_b, w_out, g_post_mix, g_pre_ffn, w_router, router_bias, w_exp_gate, w_exp_up, w_exp_down, w_sh_gate, w_sh_up, w_sh_down, g_post_ffn):
    names = ("w_ada", "b_ada", "g_pre_mix", "w_in", "conv_w", "ln_v_g", "ln_v_b", "w_spatial",
             "b_spatial", "g_out_a", "g_out_b", "w_out", "g_post_mix", "g_pre_ffn", "w_router",
             "router_bias", "w_exp_gate", "w_exp_up", "w_exp_down", "w_sh_gate", "w_sh_up",
             "w_sh_down", "g_post_ffn")
    stacked = (w_ada, b_ada, g_pre_mix, w_in, conv_w, ln_v_g, ln_v_b, w_spatial, b_spatial, g_out_a,
               g_out_b, w_out, g_post_mix, g_pre_ffn, w_router, router_bias, w_exp_gate, w_exp_up,
               w_exp_down, w_sh_gate, w_sh_up, w_sh_down, g_post_ffn)
    y_p, y_s = x_prompt, x_sample
    for l in range(w_ada.shape[0]):
        lw = {n: a[l] for n, a in zip(names, stacked)}
        y_p, y_s = _layer(y_p, y_s, c_prompt, c_sample, lw)
    return (y_p, y_s)
```

```python
import functools

import jax
import jax.numpy as jnp
from jax import lax
from jax.experimental import pallas as pl
from jax.experimental.pallas import tpu as pltpu

F32 = jnp.float32
BF16 = jnp.bfloat16
I32 = jnp.int32

D_MODEL = 2048
CONV_W = 1024
SGU_W = 1024
HEADS = 8
HEAD_DIM = 128
CHUNK = 128
N_EXP = 64
TOP_K = 8
N_GRP = 8
GRP_SZ = 8
TOP_GRP = 4
FF = 512
EPS = 1e-6
ROUTED_SCALE = 2.5
LANES = 128
ROW_SUB = D_MODEL // LANES
HALO = 8

VMEM_LIMIT = 56 * 1024 * 1024


def _rms(x, g):
    return x * lax.rsqrt(jnp.mean(x * x, axis=-1, keepdims=True) + EPS) * g


def _dot(a, b):
    return jnp.dot(a, b, preferred_element_type=F32)


def _const_spec(shape):
    nd = len(shape)
    return pl.BlockSpec(shape, lambda *_: (0,) * nd, pipeline_mode=pl.Buffered(1))


def _ada_kernel(c_ref, w_ref, b_ref, o_ref):
    c = c_ref[...]
    s = c * jax.nn.sigmoid(c)
    o_ref[...] = jnp.dot(s, w_ref[...], precision=lax.Precision.HIGHEST,
                         preferred_element_type=F32) + b_ref[...]


def _ada(c_pad, w_ada, b_ada):
    n = w_ada.shape[1]
    bn = 1536
    return pl.pallas_call(
        _ada_kernel,
        grid=(n // bn,),
        in_specs=[pl.BlockSpec(c_pad.shape, lambda j: (0, 0)),
                  pl.BlockSpec((D_MODEL, bn), lambda j: (0, j)),
                  pl.BlockSpec((1, bn), lambda j: (0, j))],
        out_specs=pl.BlockSpec((c_pad.shape[0], bn), lambda j: (0, j)),
        out_shape=jax.ShapeDtypeStruct((c_pad.shape[0], n), F32),
        compiler_params=pltpu.CompilerParams(vmem_limit_bytes=VMEM_LIMIT),
        name="ada_mod",
    )(c_pad, w_ada, b_ada.reshape(1, n))


def _mix_kernel(x_ref, xp_ref, xn_ref, mod_ref, gpre_ref, win_ref, cw_ref, lng_ref, lnb_ref,
                wsp_ref, bsp_ref, ga_ref, gb_ref, wout_ref, gpost_ref, *rest, n_i):
    o_ref, zbuf = rest[-2:]
    i = pl.program_id(1)
    tm = x_ref.shape[1]
    sh1 = mod_ref[0, 0:1, :]
    sc1 = mod_ref[0, 1:2, :]
    gt1 = mod_ref[0, 2:3, :]
    gpre = gpre_ref[...]

    def prenorm(xv):
        return _rms(xv, gpre) * (1.0 + sc1) + sh1

    x = x_ref[0]
    h = jnp.concatenate([prenorm(x), prenorm(xp_ref[0]), prenorm(xn_ref[0])], axis=0).astype(BF16)
    ht = h[:tm]

    cz = _dot(h, win_ref[:, CONV_W:3 * CONV_W])
    z = cz[:, :CONV_W] * cz[:, CONV_W:]
    zbuf[HALO:HALO + tm, :] = z[:tm]
    zbuf[0:HALO, :] = jnp.where(i == 0, 0.0, z[tm:tm + HALO])
    zbuf[HALO + tm:, :] = jnp.where(i == n_i - 1, 0.0, z[tm + HALO:])
    cw = cw_ref[...]
    conv = (zbuf[HALO - 1:HALO - 1 + tm, :] * cw[0:1] + zbuf[HALO:HALO + tm, :] * cw[1:2]
            + zbuf[HALO + 1:HALO + 1 + tm, :] * cw[2:3])
    bg = _dot(ht, win_ref[:, :CONV_W])
    ya = _rms(bg * conv, ga_ref[...])

    uv = _dot(ht, win_ref[:, 3 * CONV_W:])
    u = jax.nn.gelu(uv[:, :SGU_W], approximate=True)
    v = jax.nn.gelu(uv[:, SGU_W:], approximate=True)
    lng = lng_ref[...]
    lnb = lnb_ref[...]
    nchunk = tm // CHUNK
    parts = []
    for hd in range(HEADS):
        lo, hi = hd * HEAD_DIM, (hd + 1) * HEAD_DIM
        vh = v[:, lo:hi]
        mu = jnp.mean(vh, axis=-1, keepdims=True)
        vc = vh - mu
        var = jnp.mean(vc * vc, axis=-1, keepdims=True)
        vn = vc * lax.rsqrt(var + EPS) * lng[:, lo:hi] + lnb[:, lo:hi]
        vcat = jnp.concatenate([vn[c * CHUNK:(c + 1) * CHUNK] for c in range(nchunk)],
                               axis=1).astype(BF16)
        sp = _dot(wsp_ref[hd], vcat) + bsp_ref[:, hd:hd + 1]
        sp_rows = jnp.concatenate([sp[:, c * HEAD_DIM:(c + 1) * HEAD_DIM] for c in range(nchunk)],
                                  axis=0)
        parts.append(u[:, lo:hi] * sp_rows)
    yb = _rms(jnp.concatenate(parts, axis=1), gb_ref[...])

    cat = jnp.concatenate([ya, yb], axis=1).astype(BF16)
    mix = _dot(cat, wout_ref[...])
    o_ref[0] = x + gt1 * _rms(mix, gpost_ref[...])


def _mix(x, mod, w, tm, b0, b_total, x1_prev=None):
    b, s, d = x.shape
    n_i = s // tm
    hb = tm // HALO
    n_hb = s // HALO
    row = lambda a: a.reshape(1, -1)
    extra_specs, extra_args, aliases = [], [], {}
    if x1_prev is not None:
        extra_specs, extra_args, aliases = [pl.BlockSpec(memory_space=pl.ANY)], [x1_prev], {15: 0}
    return pl.pallas_call(
        functools.partial(_mix_kernel, n_i=n_i),
        grid=(b, n_i),
        in_specs=[
            pl.BlockSpec((1, tm, d), lambda bb, i: (bb, i, 0)),
            pl.BlockSpec((1, HALO, d), lambda bb, i: (bb, jnp.maximum(i * hb - 1, 0), 0)),
            pl.BlockSpec((1, HALO, d), lambda bb, i: (bb, jnp.minimum((i + 1) * hb, n_hb - 1), 0)),
            pl.BlockSpec((1, 6, d), lambda bb, i: (b0 + bb, 0, 0)),
            _const_spec((1, d)),
            _const_spec(w["w_in"].shape),
            _const_spec((3, CONV_W)),
            _const_spec((1, SGU_W)),
            _const_spec((1, SGU_W)),
            _const_spec((HEADS, CHUNK, CHUNK)),
            _const_spec((CHUNK, HEADS)),
            _const_spec((1, CONV_W)),
            _const_spec((1, SGU_W)),
            _const_spec((d, d)),
            _const_spec((1, d)),
        ] + extra_specs,
        out_specs=pl.BlockSpec((1, tm, d), lambda bb, i: (b0 + bb, i, 0)),
        out_shape=jax.ShapeDtypeStruct((b_total, s, d), F32),
        input_output_aliases=aliases,
        scratch_shapes=[pltpu.VMEM((tm + 2 * HALO, CONV_W), F32)],
        compiler_params=pltpu.CompilerParams(
            dimension_semantics=("arbitrary", "arbitrary"), vmem_limit_bytes=VMEM_LIMIT),
        name="token_mix",
    )(x, x, x, mod, row(w["g_pre_mix"]), w["w_in"], w["conv_w"], row(w["ln_v_g"]), row(w["ln_v_b"]),
      w["w_spatial"], w["b_spatial_t"], row(w["g_out_a"]), row(w["g_out_b"]), w["w_out"],
      row(w["g_post_mix"]), *extra_args)


def _router_kernel(x_ref, mod_ref, g_ref, whi_ref, wlo_ref, br_ref,
                   idx_ref, pos_ref, gate_ref, cnt_ref, carry):
    first = jnp.logical_and(pl.program_id(0) == 0, pl.program_id(1) == 0)

    @pl.when(first)
    def _():
        carry[...] = jnp.zeros_like(carry)

    tm = x_ref.shape[1]
    sh2 = mod_ref[0, 3:4, :]
    sc2 = mod_ref[0, 4:5, :]
    h2 = _rms(x_ref[0], g_ref[...]) * (1.0 + sc2) + sh2

    h_hi = h2.astype(BF16)
    h_lo = (h2 - h_hi.astype(F32)).astype(BF16)
    w_hi = whi_ref[...]
    logits_t = _dot(h_hi, w_hi) + (_dot(h_lo, w_hi) + _dot(h_hi, wlo_ref[...]))
    logits = logits_t.T
    scores = jax.nn.sigmoid(logits)
    sel = scores + br_ref[...]

    ri = lax.broadcasted_iota(I32, (N_EXP, LANES), 0)
    e_map = (ri % N_GRP) * GRP_SZ + ri // N_GRP
    gi = lax.broadcasted_iota(I32, (N_GRP, LANES), 0)

    n_chunk = tm // LANES
    sel_chunks, idx_chunks, score_chunks = [], [], []
    for c in range(n_chunk):
        sel_c = sel[:, c * LANES:(c + 1) * LANES]
        sc_c = scores[:, c * LANES:(c + 1) * LANES]
        m1 = sel_c[0:N_GRP]
        m2 = jnp.full_like(m1, -jnp.inf)
        for j in range(1, GRP_SZ):
            vj = sel_c[j * N_GRP:(j + 1) * N_GRP]
            m2 = jnp.maximum(m2, jnp.minimum(m1, vj))
            m1 = jnp.maximum(m1, vj)
        gs = m1 + m2
        grank = jnp.zeros(gs.shape, I32)
        for g2 in range(N_GRP):
            rowv = gs[g2:g2 + 1, :]
            beats = jnp.logical_or(rowv > gs, jnp.logical_and(rowv == gs, g2 < gi))
            grank = grank + beats.astype(I32)
        gmask = grank < TOP_GRP
        m = jnp.concatenate(
            [jnp.where(gmask, sel_c[j * N_GRP:(j + 1) * N_GRP], -jnp.inf) for j in range(GRP_SZ)],
            axis=0)
        chosen = jnp.zeros(m.shape, jnp.bool_)
        idx_rows, score_rows = [], []
        for _ in range(TOP_K):
            mx = jnp.max(m, axis=0, keepdims=True)
            e_min = jnp.min(jnp.where(m == mx, e_map, N_EXP), axis=0, keepdims=True)
            pick = e_map == e_min
            chosen = jnp.logical_or(chosen, pick)
            m = jnp.where(pick, -jnp.inf, m)
            idx_rows.append(e_min)
            score_rows.append(jnp.sum(jnp.where(pick, sc_c, 0.0), axis=0, keepdims=True))
        sel_chunks.append(chosen.astype(F32))
        idx_chunks.append(jnp.concatenate(idx_rows, axis=0))
        score_chunks.append(jnp.concatenate(score_rows, axis=0))
    self32 = jnp.concatenate(sel_chunks, axis=1)
    idx_k = jnp.concatenate(idx_chunks, axis=1)
    score_k = jnp.concatenate(score_chunks, axis=1)
    wsum = jnp.sum(score_k, axis=0, keepdims=True)
    gate_ref[...] = score_k / (wsum + 1e-20) * ROUTED_SCALE
    idx_ref[...] = idx_k

    selb = self32.astype(BF16)
    tr = lax.broadcasted_iota(I32, (tm, tm), 0)
    tc = lax.broadcasted_iota(I32, (tm, tm), 1)
    upper = (tr <= tc).astype(BF16)
    incl = _dot(selb, upper)
    pos_full = incl - self32 + carry[...]
    carry[...] = carry[...] + incl[:, tm - 1:tm]

    pos_chunks = []
    for c in range(n_chunk):
        pos_c = pos_full[:, c * LANES:(c + 1) * LANES]
        rows = []
        for k in range(TOP_K):
            pick = e_map == idx_k[k:k + 1, c * LANES:(c + 1) * LANES]
            rows.append(jnp.sum(jnp.where(pick, pos_c, 0.0), axis=0, keepdims=True))
        pos_chunks.append(jnp.concatenate(rows, axis=0))
    pos_ref[...] = jnp.concatenate(pos_chunks, axis=1).astype(I32)
    cnt_ref[...] = jnp.broadcast_to(carry[...], cnt_ref.shape)


def _router(x1, mod, g_pre_ffn, w_hi, w_lo, br, tm):
    b, s, d = x1.shape
    n_i = s // tm
    t = b * s
    return pl.pallas_call(
        _router_kernel,
        grid=(b, n_i),
        in_specs=[
            pl.BlockSpec((1, tm, d), lambda bb, i: (bb, i, 0)),
            pl.BlockSpec((1, 6, d), lambda bb, i: (bb, 0, 0)),
            _const_spec((1, d)),
            _const_spec((d, N_EXP)),
            _const_spec((d, N_EXP)),
            _const_spec((N_EXP, 1)),
        ],
        out_specs=[
            pl.BlockSpec((TOP_K, tm), lambda bb, i: (0, bb * n_i + i)),
            pl.BlockSpec((TOP_K, tm), lambda bb, i: (0, bb * n_i + i)),
            pl.BlockSpec((TOP_K, tm), lambda bb, i: (0, bb * n_i + i)),
            pl.BlockSpec((N_EXP, LANES), lambda bb, i: (0, 0)),
        ],
        out_shape=[
            jax.ShapeDtypeStruct((TOP_K, t), I32),
            jax.ShapeDtypeStruct((TOP_K, t), I32),
            jax.ShapeDtypeStruct((TOP_K, t), F32),
            jax.ShapeDtypeStruct((N_EXP, LANES), F32),
        ],
        scratch_shapes=[pltpu.VMEM((N_EXP, 1), F32)],
        compiler_params=pltpu.CompilerParams(
            dimension_semantics=("arbitrary", "arbitrary"), vmem_limit_bytes=VMEM_LIMIT),
        name="router",
    )(x1, mod, g_pre_ffn.reshape(1, d), w_hi, w_lo, br)


def _dispatch_kernel(dest_ref, x_ref, mod_ref, g_ref, wg_ref, wu_ref, wd_ref, xs_ref, sh_ref,
                     rowbuf, sem, *, n_b, n_i):
    td = x_ref.shape[1]
    n = pl.program_id(0) * n_i + pl.program_id(1)
    total = n_b * n_i
    n_rows = td * TOP_K * ROW_SUB

    def drain(slot):
        pltpu.make_async_copy(xs_ref.at[pl.ds(0, n_rows)], xs_ref.at[pl.ds(0, n_rows)],
                              sem.at[slot]).wait()

    def step(slot):
        @pl.when(n >= 2)
        def _():
            drain(slot)

        sh2 = mod_ref[0, 3:4, :]
        sc2 = mod_ref[0, 4:5, :]
        h2 = _rms(x_ref[0], g_ref[...]) * (1.0 + sc2) + sh2
        for s in range(ROW_SUB):
            rowbuf[slot, pl.ds(s, td, stride=ROW_SUB), :] = h2[:, s * LANES:(s + 1) * LANES]

        def body(j, c):
            src = rowbuf.at[slot, pl.ds(j * ROW_SUB, ROW_SUB)]
            for k in range(TOP_K):
                dst = dest_ref[0, 0, j * TOP_K + k]
                pltpu.make_async_copy(src, xs_ref.at[pl.ds(dst * ROW_SUB, ROW_SUB)],
                                      sem.at[slot]).start(priority=k % 2)
            return c
        lax.fori_loop(0, td, body, 0, unroll=2)

        hb = h2.astype(BF16)
        g = _dot(hb, wg_ref[...])
        u = _dot(hb, wu_ref[...])
        sh_ref[0] = _dot((g * jax.nn.sigmoid(g) * u).astype(BF16), wd_ref[...])

        @pl.when(n == total - 1)
        def _():
            drain(slot)
            if total >= 2:
                drain(1 - slot)

    @pl.when(n % 2 == 0)
    def _():
        step(0)

    @pl.when(n % 2 == 1)
    def _():
        step(1)


def _dispatch(dest_flat, x1, mod, w, n_slots, td):
    b, s, d = x1.shape
    n_i = s // td
    dest3 = dest_flat.reshape(b * n_i, 1, td * TOP_K)
    return pl.pallas_call(
        functools.partial(_dispatch_kernel, n_b=b, n_i=n_i),
        grid=(b, n_i),
        in_specs=[pl.BlockSpec((1, 1, td * TOP_K), lambda bb, i: (bb * n_i + i, 0, 0),
                               memory_space=pltpu.SMEM),
                  pl.BlockSpec((1, td, d), lambda bb, i: (bb, i, 0)),
                  pl.BlockSpec((1, 6, d), lambda bb, i: (bb, 0, 0)),
                  _const_spec((1, d)),
                  _const_spec((d, FF)),
                  _const_spec((d, FF)),
                  _const_spec((FF, d))],
        out_specs=[pl.BlockSpec(memory_space=pl.ANY),
                   pl.BlockSpec((1, td, d), lambda bb, i: (bb, i, 0))],
        out_shape=[jax.ShapeDtypeStruct((n_slots * ROW_SUB, LANES), F32),
                   jax.ShapeDtypeStruct((b, s, d), F32)],
        scratch_shapes=[pltpu.VMEM((2, td * ROW_SUB, LANES), F32),
                        pltpu.SemaphoreType.DMA((2,))],
        compiler_params=pltpu.CompilerParams(
            dimension_semantics=("arbitrary", "arbitrary"), vmem_limit_bytes=VMEM_LIMIT),
        name="dispatch",
    )(dest3, x1, mod, w["g_pre_ffn"].reshape(1, d), w["w_sh_gate"], w["w_sh_up"], w["w_sh_down"])


def _expert_kernel(be_ref, nv_ref, nu_ref, xs_ref, wg_ref, wu_ref, wd_ref, ys_ref,
                   wg_s, wu_s, wd_s):
    i = pl.program_id(0)
    tme = xs_ref.shape[0] // ROW_SUB

    @pl.when(i < nu_ref[0])
    def _():
        prev = be_ref[jnp.maximum(i - 1, 0)]
        changed = jnp.logical_or(i == 0, prev != be_ref[i])

        @pl.when(changed)
        def _():
            wg_s[...] = wg_ref[0].astype(BF16)
            wu_s[...] = wu_ref[0].astype(BF16)
            wd_s[...] = wd_ref[0].astype(BF16)

        x = jnp.concatenate([xs_ref[pl.ds(s, tme, stride=ROW_SUB), :] for s in range(ROW_SUB)],
                            axis=1)
        rows = lax.broadcasted_iota(I32, (tme, 1), 0)
        x = jnp.where(rows < nv_ref[i], x, 0.0).astype(BF16)
        g = _dot(x, wg_s[...])
        u = _dot(x, wu_s[...])
        a = (g * jax.nn.sigmoid(g) * u).astype(BF16)
        y = _dot(a, wd_s[...])
        for s in range(ROW_SUB):
            ys_ref[pl.ds(s, tme, stride=ROW_SUB), :] = y[:, s * LANES:(s + 1) * LANES]


def _experts(block_e, nvalid, nused, xs, wg, wu, wd, tme):
    n_blocks = xs.shape[0] // (tme * ROW_SUB)
    blk = lambda i, be, nv, nu: (jnp.minimum(i, nu[0] - 1), 0)
    wsel = lambda i, be, nv, nu: (be[i], 0, 0)
    return pl.pallas_call(
        _expert_kernel,
        grid_spec=pltpu.PrefetchScalarGridSpec(
            num_scalar_prefetch=3,
            grid=(n_blocks,),
            in_specs=[pl.BlockSpec((tme * ROW_SUB, LANES), blk),
                      pl.BlockSpec((1, D_MODEL, FF), wsel),
                      pl.BlockSpec((1, D_MODEL, FF), wsel),
                      pl.BlockSpec((1, FF, D_MODEL), wsel)],
            out_specs=pl.BlockSpec((tme * ROW_SUB, LANES), blk),
            scratch_shapes=[pltpu.VMEM((D_MODEL, FF), BF16),
                            pltpu.VMEM((D_MODEL, FF), BF16),
                            pltpu.VMEM((FF, D_MODEL), BF16)],
        ),
        out_shape=jax.ShapeDtypeStruct(xs.shape, F32),
        compiler_params=pltpu.CompilerParams(
            dimension_semantics=("arbitrary",), vmem_limit_bytes=VMEM_LIMIT),
        name="experts",
    )(block_e, nvalid, nused, xs, wg, wu, wd)


N_GATHER_BUF = 3


def _combine_kernel(dest_ref, dest1_ref, dest2_ref, gexp_ref, x_ref, sh_ref, mod_ref, gpost_ref,
                    ys_ref, o_ref, buf, rsum, sem, *, n_b, n_i):
    tc = x_ref.shape[1]
    n = pl.program_id(0) * n_i + pl.program_id(1)
    total = n_b * n_i
    n_rows = tc * TOP_K * ROW_SUB
    half = ROW_SUB // 2

    def gather_rows(idx_ref, slot, unroll):
        def body(j, c):
            for k in range(TOP_K):
                src = idx_ref[0, 0, j * TOP_K + k]
                pltpu.make_async_copy(ys_ref.at[pl.ds(src * ROW_SUB, ROW_SUB)],
                                      buf.at[slot, pl.ds((j * TOP_K + k) * ROW_SUB, ROW_SUB)],
                                      sem.at[slot]).start(priority=k % 2)
            return c
        lax.fori_loop(0, tc, body, 0, unroll=unroll)

    def drain(slot):
        pltpu.make_async_copy(ys_ref.at[pl.ds(0, n_rows)], buf.at[slot], sem.at[slot]).wait()

    def step(slot):
        slot1 = (slot + 1) % N_GATHER_BUF
        slot2 = (slot + 2) % N_GATHER_BUF

        @pl.when(n == 0)
        def _():
            gather_rows(dest_ref, slot, 2)
            gather_rows(dest1_ref, slot1, 2)

        drain(slot)
        gather_rows(dest2_ref, slot2, True)

        for j in range(tc):
            base = j * (TOP_K * ROW_SUB)
            gk = jnp.broadcast_to(gexp_ref[pl.ds(j * TOP_K, 1), :], (half, LANES))
            acc0 = gk * buf[slot, pl.ds(base, half), :]
            acc1 = gk * buf[slot, pl.ds(base + half, half), :]
            for k in range(1, TOP_K):
                gk = jnp.broadcast_to(gexp_ref[pl.ds(j * TOP_K + k, 1), :], (half, LANES))
                acc0 = acc0 + gk * buf[slot, pl.ds(base + k * ROW_SUB, half), :]
                acc1 = acc1 + gk * buf[slot, pl.ds(base + k * ROW_SUB + half, half), :]
            rsum[pl.ds(j * ROW_SUB, half), :] = acc0
            rsum[pl.ds(j * ROW_SUB + half, half), :] = acc1

        routed = jnp.concatenate([rsum[pl.ds(s, tc, stride=ROW_SUB), :] for s in range(ROW_SUB)],
                                 axis=1)
        ffn = routed + sh_ref[0]
        gt2 = mod_ref[0, 5:6, :]
        o_ref[0] = x_ref[0] + gt2 * _rms(ffn, gpost_ref[...])

        @pl.when(n + 1 == total)
        def _():
            drain(slot1)
            drain(slot2)

    for slot in range(N_GATHER_BUF):
        pl.when(n % N_GATHER_BUF == slot)(functools.partial(step, slot))


def _combine(dest_flat, gates_rows,b0, b, x1, shared, mod, w, ys, tc):
    _, s, d = x1.shape
    n_i = s // tc
    t_all = dest_flat.shape[0] // TOP_K
    dest3 = dest_flat.reshape(t_all // tc, 1, tc * TOP_K)
    t0 = b0 * n_i
    last = t0 + b * n_i - 1
    cur = lambda bb, i: (t0 + bb * n_i + i, 0, 0)
    nxt1 = lambda bb, i: (jnp.minimum(t0 + bb * n_i + i + 1, last), 0, 0)
    nxt2 = lambda bb, i: (jnp.minimum(t0 + bb * n_i + i + 2, last), 0, 0)
    return pl.pallas_call(
        functools.partial(_combine_kernel, n_b=b, n_i=n_i),
        grid=(b, n_i),
        in_specs=[
            pl.BlockSpec((1, 1, tc * TOP_K), cur, memory_space=pltpu.SMEM),
            pl.BlockSpec((1, 1, tc * TOP_K), nxt1, memory_space=pltpu.SMEM),
            pl.BlockSpec((1, 1, tc * TOP_K), nxt2, memory_space=pltpu.SMEM),
            pl.BlockSpec((tc * TOP_K, LANES), lambda bb, i: (t0 + bb * n_i + i, 0)),
            pl.BlockSpec((1, tc, d), lambda bb, i: (b0 + bb, i, 0)),
            pl.BlockSpec((1, tc, d), lambda bb, i: (b0 + bb, i, 0)),
            pl.BlockSpec((1, 6, d), lambda bb, i: (b0 + bb, 0, 0)),
            _const_spec((1, d)),
            pl.BlockSpec(memory_space=pl.ANY),
        ],
        out_specs=pl.BlockSpec((1, tc, d), lambda bb, i: (bb, i, 0)),
        out_shape=jax.ShapeDtypeStruct((b, s, d), F32),
        scratch_shapes=[pltpu.VMEM((N_GATHER_BUF, tc * TOP_K * ROW_SUB, LANES), F32),
                        pltpu.VMEM((tc * ROW_SUB, LANES), F32),
                        pltpu.SemaphoreType.DMA((N_GATHER_BUF,))],
        compiler_params=pltpu.CompilerParams(
            dimension_semantics=("arbitrary", "arbitrary"), vmem_limit_bytes=VMEM_LIMIT),
        name="combine",
    )(dest3, dest3, dest3, gates_rows, x1, shared, mod, w["g_post_ffn"].reshape(1, d), ys)


def _tile_sizes(s):
    return dict(tm_mix=min(256, s), tm_route=min(512, s), td=min(256, s), tme=min(512, s),
                tc=min(128, s))


def _layer(x_prompt, x_sample, c_prompt, c_sample, lw):
    bp, s, d = x_prompt.shape
    bs = x_sample.shape[0]
    ts = _tile_sizes(s)

    nb = bp + bs
    nb_pad = -(-nb // 8) * 8
    c_all = jnp.concatenate([c_prompt, c_sample, jnp.zeros((nb_pad - nb, d), F32)], axis=0)
    mod = _ada(c_all, lw["w_ada"], lw["b_ada"])[:nb].reshape(nb, 6, d)

    w = dict(lw)
    w["w_in"] = lw["w_in"].astype(BF16)
    w["w_out"] = lw["w_out"].astype(BF16)
    w["w_spatial"] = lw["w_spatial"].astype(BF16)
    w["b_spatial_t"] = lw["b_spatial"].T
    w["w_sh_gate"] = lw["w_sh_gate"].astype(BF16)
    w["w_sh_up"] = lw["w_sh_up"].astype(BF16)
    w["w_sh_down"] = lw["w_sh_down"].astype(BF16)

    x1 = _mix(x_prompt, mod, w, ts["tm_mix"], 0, nb)
    x1 = _mix(x_sample, mod, w, ts["tm_mix"], bp, nb, x1_prev=x1)

    perm = (jnp.arange(N_EXP) % N_GRP) * GRP_SZ + jnp.arange(N_EXP) // N_GRP
    wr = lw["w_router"][:, perm]
    wr_hi = wr.astype(BF16)
    wr_lo = (wr - wr_hi.astype(F32)).astype(BF16)
    br = lw["router_bias"][perm].reshape(N_EXP, 1)
    idx, pos, gates, cnt_all = _router(x1, mod, lw["g_pre_ffn"], wr_hi, wr_lo, br, ts["tm_route"])

    tme = ts["tme"]
    t_all = nb * s
    counts = cnt_all[:, 0].astype(I32).reshape(GRP_SZ, N_GRP).T.reshape(N_EXP)
    padded = (counts + tme - 1) // tme * tme
    pad_end = jnp.cumsum(padded)
    off = pad_end - padded
    gates_rows = jnp.broadcast_to(gates.T.reshape(t_all * TOP_K, 1), (t_all * TOP_K, LANES))
    e_ids = jnp.arange(N_EXP, dtype=I32)
    off_of_idx = jnp.sum(jnp.where(idx[:, :, None] == e_ids, off, 0), axis=-1)
    dest_flat = (off_of_idx + pos).T.reshape(t_all * TOP_K)
    n_blocks = -(-t_all * TOP_K // tme) + N_EXP
    nused = (pad_end[-1] // tme).astype(I32)
    blk_ids = jnp.minimum(jnp.arange(n_blocks, dtype=I32), nused - 1)
    blk_start = blk_ids * tme
    block_e = jnp.minimum(jnp.sum((pad_end[None, :] <= blk_start[:, None]).astype(I32), axis=1),
                          N_EXP - 1)
    be_hot = block_e[:, None] == e_ids
    cnt_of_blk = jnp.sum(jnp.where(be_hot, counts, 0), axis=1)
    off_of_blk = jnp.sum(jnp.where(be_hot, off, 0), axis=1)
    nvalid = jnp.clip(cnt_of_blk - (blk_start - off_of_blk), 0, tme).astype(I32)

    xs, shared = _dispatch(dest_flat, x1, mod, w, n_blocks * tme, ts["td"])
    ys = _experts(block_e, nvalid, nused.reshape(1), xs, lw["w_exp_gate"], lw["w_exp_up"],
                  lw["w_exp_down"], tme)

    y_p = _combine(dest_flat, gates_rows,0, bp, x1, shared, mod, w, ys, ts["tc"])
    y_s = _combine(dest_flat, gates_rows,bp, bs, x1, shared, mod, w, ys, ts["tc"])
    return y_p, y_s


def kernel(x_prompt, x_sample, c_prompt, c_sample, w_ada, b_ada, g_pre_mix, w_in, conv_w, ln_v_g, ln_v_b, w_spatial, b_spatial, g_out_a, g_out_b, w_out, g_post_mix, g_pre_ffn, w_router, router_bias, w_exp_gate, w_exp_up, w_exp_down, w_sh_gate, w_sh_up, w_sh_down, g_post_ffn):
    names = ("w_ada", "b_ada", "g_pre_mix", "w_in", "conv_w", "ln_v_g", "ln_v_b", "w_spatial",
             "b_spatial", "g_out_a", "g_out_b", "w_out", "g_post_mix", "g_pre_ffn", "w_router",
             "router_bias", "w_exp_gate", "w_exp_up", "w_exp_down", "w_sh_gate", "w_sh_up",
             "w_sh_down", "g_post_ffn")
    stacked = (w_ada, b_ada, g_pre_mix, w_in, conv_w, ln_v_g, ln_v_b, w_spatial, b_spatial, g_out_a,
               g_out_b, w_out, g_post_mix, g_pre_ffn, w_router, router_bias, w_exp_gate, w_exp_up,
               w_exp_down, w_sh_gate, w_sh_up, w_sh_down, g_post_ffn)
    y_p, y_s = x_prompt, x_sample
    for l in range(w_ada.shape[0]):
        lw = {n: a[l] for n, a in zip(names, stacked)}
        y_p, y_s = _layer(y_p, y_s, c_prompt, c_sample, lw)
    return (y_p, y_s)
```
